```python
import jax, jax.numpy as jnp
from jax import lax
import numpy as np

D_MODEL = 1024
BATCH = 4
SEQ = 8192
DEPTH = 2

GRID_W = 64
HEAD_DIM = 64
Q_BLOCK = 128
ROPE_THETA = 10000.0
EPS = 1e-6
A_HEADS = 8
A_KV_HEADS = 2
B_HEADS = 8
B_Q_RANK = 384
B_KV_RANK = 256
B_NOPE = 64
B_ROPE = 32
B_V = 64
C_HEADS = 16
C_WIN_ROWS = 8
C_WIN_COLS = 16
D_FF = 4 * D_MODEL

A_Q = A_HEADS * HEAD_DIM
A_KV = A_KV_HEADS * HEAD_DIM
EVEN_SPLITS = (A_Q, A_Q + A_KV, A_Q + 2 * A_KV, A_Q + 2 * A_KV + B_Q_RANK,
               A_Q + 2 * A_KV + B_Q_RANK + B_KV_RANK)
EVEN_IN = A_Q + 2 * A_KV + B_Q_RANK + B_KV_RANK + B_ROPE
MIX_WIDTH = A_HEADS * HEAD_DIM + B_HEADS * B_V
C_WIDTH = C_HEADS * HEAD_DIM
N_EVEN = (DEPTH + 1) // 2
N_ODD = DEPTH // 2

kernel_name = "hybrid_gqa_mla_natten_encoder"


def rmsnorm(x, g):
    xf = x.astype(jnp.float32)
    y = xf * lax.rsqrt(jnp.mean(xf * xf, axis=-1, keepdims=True) + EPS)
    return (y * g.astype(jnp.float32)).astype(x.dtype)


def rope_1d(x, pos):
    d = x.shape[-1]
    inv = ROPE_THETA ** (-jnp.arange(0, d, 2, dtype=jnp.float32) / d)
    ang = pos[:, None] * inv[None, :]
    cos = jnp.cos(ang)[:, None, :].astype(x.dtype)
    sin = jnp.sin(ang)[:, None, :].astype(x.dtype)
    x1, x2 = jnp.split(x, 2, axis=-1)
    return jnp.concatenate([x1 * cos - x2 * sin, x2 * cos + x1 * sin], axis=-1)


def axial_rope(x, row, col):
    xr, xc = jnp.split(x, 2, axis=-1)
    return jnp.concatenate([rope_1d(xr, row), rope_1d(xc, col)], axis=-1)


def block_attention(q, k, v, scale):
    b, s, hq, d = q.shape
    hkv = k.shape[2]
    g = hq // hkv
    nblk = s // Q_BLOCK
    qb = q.reshape(b, nblk, Q_BLOCK, hkv, g, d).transpose(1, 0, 3, 4, 2, 5)
    kt = k.transpose(0, 2, 1, 3)
    vt = v.transpose(0, 2, 1, 3)

    def one(qblk):
        sc = jnp.einsum('bkgqd,bksd->bkgqs', qblk, kt).astype(jnp.float32) * scale
        p = jax.nn.softmax(sc, axis=-1).astype(vt.dtype)
        return jnp.einsum('bkgqs,bksd->bkgqd', p, vt)

    o = lax.map(one, qb)
    return o.transpose(1, 0, 4, 2, 3, 5).reshape(b, s, hq, -1)


def hybrid_attention(h, w_in, a_q_norm, a_k_norm, b_q_norm, b_w_uq, b_kv_norm, b_w_ukv,
                     w_out, row, col):
    b, s, _ = h.shape
    z = h @ w_in
    qa, ka, va, cq, ckv, kr = jnp.split(z, EVEN_SPLITS, axis=-1)
    qa = axial_rope(rmsnorm(qa.reshape(b, s, A_HEADS, HEAD_DIM), a_q_norm), row, col)
    ka = axial_rope(rmsnorm(ka.reshape(b, s, A_KV_HEADS, HEAD_DIM), a_k_norm), row, col)
    va = va.reshape(b, s, A_KV_HEADS, HEAD_DIM)
    oa = block_attention(qa, ka, va, HEAD_DIM ** -0.5)
    qb = (rmsnorm(cq, b_q_norm) @ b_w_uq).reshape(b, s, B_HEADS, B_NOPE + B_ROPE)
    q_nope, q_rope = jnp.split(qb, (B_NOPE,), axis=-1)
    q_rope = axial_rope(q_rope, row, col)
    kvb = (rmsnorm(ckv, b_kv_norm) @ b_w_ukv).reshape(b, s, B_HEADS, B_NOPE + B_V)
    k_nope, vb = jnp.split(kvb, (B_NOPE,), axis=-1)
    k_rope = axial_rope(kr[:, :, None, :], row, col)
    q_full = jnp.concatenate([q_nope, q_rope], axis=-1)
    k_full = jnp.concatenate([k_nope, jnp.broadcast_to(k_rope, (b, s, B_HEADS, B_ROPE))], axis=-1)
    ob = block_attention(q_full, k_full, vb, (B_NOPE + B_ROPE) ** -0.5)
    mixed = jnp.concatenate([oa.reshape(b, s, -1), ob.reshape(b, s, -1)], axis=-1)
    return mixed @ w_out


def neighbourhood_attention(h, w_qkv, rpb, w_out):
    b, s, _ = h.shape
    rows = s // GRID_W
    wr = min(C_WIN_ROWS, rows)
    wc = C_WIN_COLS
    nblk = s // Q_BLOCK
    q, k, v = jnp.split(h @ w_qkv, 3, axis=-1)
    q = q.reshape(b, s, C_HEADS, HEAD_DIM).transpose(0, 2, 1, 3)
    k = k.reshape(b, s, C_HEADS, HEAD_DIM).transpose(0, 2, 1, 3)
    v = v.reshape(b, s, C_HEADS, HEAD_DIM).transpose(0, 2, 1, 3)
    t = jnp.arange(s, dtype=jnp.int32)
    qr, qc = t // GRID_W, t % GRID_W
    rs = jnp.clip(qr - wr // 2, 0, rows - wr)
    cs = jnp.clip(qc - wc // 2, 0, GRID_W - wc)
    kr = rs[:, None, None] + jnp.arange(wr, dtype=jnp.int32)[None, :, None]
    kc = cs[:, None, None] + jnp.arange(wc, dtype=jnp.int32)[None, None, :]
    idx = (kr * GRID_W + kc).reshape(s, wr * wc)
    ridx = jnp.broadcast_to(kr - qr[:, None, None] + (C_WIN_ROWS - 1), (s, wr, wc)).reshape(s, -1)
    cidx = jnp.broadcast_to(kc - qc[:, None, None] + (C_WIN_COLS - 1), (s, wr, wc)).reshape(s, -1)
    qbk = q.reshape(b, C_HEADS, nblk, Q_BLOCK, HEAD_DIM).transpose(2, 0, 1, 3, 4)
    blk = lambda a: a.reshape(nblk, Q_BLOCK, -1)
    scale = HEAD_DIM ** -0.5

    def one(args):
        qblk, iblk, rblk, cblk = args
        kg = jnp.take(k, iblk, axis=2)
        vg = jnp.take(v, iblk, axis=2)
        bias = rpb[:, rblk, cblk].astype(jnp.float32)
        sc = jnp.einsum('bhqd,bhqkd->bhqk', qblk, kg).astype(jnp.float32) * scale + bias[None]
        p = jax.nn.softmax(sc, axis=-1).astype(vg.dtype)
        return jnp.einsum('bhqk,bhqkd->bhqd', p, vg)

    o = lax.map(one, (qbk, blk(idx), blk(ridx), blk(cidx)))
    o = o.transpose(1, 0, 3, 2, 4).reshape(b, s, C_WIDTH)
    return o @ w_out


def sq_relu_mlp(h, w_up, w_down):
    u = jax.nn.relu(h @ w_up)
    return (u * u) @ w_down


def setup_inputs(seed: int = 0) -> dict:
    key = jax.random.key(seed)
    ks = jax.random.split(key, 20)

    def w(k, shape, fan_in):
        return jax.random.normal(k, shape, jnp.float32) * (fan_in ** -0.5)

    def gain(k, shape):
        return 1.0 + 0.05 * jax.random.normal(k, shape, jnp.float32)

    return {
        "x": jax.random.normal(ks[0], (BATCH, SEQ, D_MODEL), jnp.float32),
        "norm_mix": gain(ks[1], (DEPTH, D_MODEL)),
        "ev_w_in": w(ks[2], (N_EVEN, D_MODEL, EVEN_IN), D_MODEL),
        "ev_a_q_norm": gain(ks[3], (N_EVEN, HEAD_DIM)),
        "ev_a_k_norm": gain(ks[4], (N_EVEN, HEAD_DIM)),
        "ev_b_q_norm": gain(ks[5], (N_EVEN, B_Q_RANK)),
        "ev_b_w_uq": w(ks[6], (N_EVEN, B_Q_RANK, B_HEADS * (B_NOPE + B_ROPE)), B_Q_RANK),
        "ev_b_kv_norm": gain(ks[7], (N_EVEN, B_KV_RANK)),
        "ev_b_w_ukv": w(ks[8], (N_EVEN, B_KV_RANK, B_HEADS * (B_NOPE + B_V)), B_KV_RANK),
        "ev_w_out": w(ks[9], (N_EVEN, MIX_WIDTH, D_MODEL), MIX_WIDTH),
        "od_w_qkv": w(ks[10], (N_ODD, D_MODEL, 3 * C_WIDTH), D_MODEL),
        "od_rpb": 0.1 * jax.random.normal(ks[11], (N_ODD, C_HEADS, 2 * C_WIN_ROWS - 1, 2 * C_WIN_COLS - 1), jnp.float32),
        "od_w_out": w(ks[12], (N_ODD, C_WIDTH, D_MODEL), C_WIDTH),
        "norm_ffn": gain(ks[13], (DEPTH, D_MODEL)),
        "ffn_w_up": w(ks[14], (DEPTH, D_MODEL, D_FF), D_MODEL),
        "ffn_w_down": w(ks[15], (DEPTH, D_FF, D_MODEL), D_FF),
        "final_norm": gain(ks[16], (D_MODEL,)),
    }


def reference(x, norm_mix, ev_w_in, ev_a_q_norm, ev_a_k_norm, ev_b_q_norm, ev_b_w_uq,
              ev_b_kv_norm, ev_b_w_ukv, ev_w_out, od_w_qkv, od_rpb, od_w_out,
              norm_ffn, ffn_w_up, ffn_w_down, final_norm):
    s = x.shape[1]
    t = jnp.arange(s, dtype=jnp.int32)
    row = (t // GRID_W).astype(jnp.float32)
    col = (t % GRID_W).astype(jnp.float32)
    h = x
    for layer in range(DEPTH):
        i = layer // 2
        hn = rmsnorm(h, norm_mix[layer])
        if layer % 2 == 0:
            h = h + hybrid_attention(hn, ev_w_in[i], ev_a_q_norm[i], ev_a_k_norm[i],
                                     ev_b_q_norm[i], ev_b_w_uq[i], ev_b_kv_norm[i],
                                     ev_b_w_ukv[i], ev_w_out[i], row, col)
        else:
            h = h + neighbourhood_attention(hn, od_w_qkv[i], od_rpb[i], od_w_out[i])
        h = h + sq_relu_mlp(rmsnorm(h, norm_ffn[layer]), ffn_w_up[layer], ffn_w_down[layer])
    return rmsnorm(h, final_norm)
```

```python
import functools
import math

import jax
import jax.numpy as jnp
from jax import lax
from jax.experimental import pallas as pl
from jax.experimental.pallas import tpu as pltpu

D_MODEL = 1024
GRID_W = 64
HEAD_DIM = 64
ROPE_THETA = 10000.0
EPS = 1e-6
A_HEADS = 8
A_KV_HEADS = 2
B_HEADS = 8
B_Q_RANK = 384
B_KV_RANK = 256
B_NOPE = 64
B_ROPE = 32
B_V = 64
C_HEADS = 16
C_WIN_ROWS = 8
C_WIN_COLS = 16
D_FF = 4 * D_MODEL

A_Q = A_HEADS * HEAD_DIM
A_KV = A_KV_HEADS * HEAD_DIM
B_QK_PAD = 128
LOG2E = math.log2(math.e)
NEG = -1e30

BF16 = jnp.bfloat16
F32 = jnp.float32

VMEM_LIMIT = 56 * 1024 * 1024

PROJ_TM = 512
MLP_TM = 512
FLASH_TK = 512
FLASH_M = 1024
NAT_QROWS = 8
NAT_KROWS = 16


def _const_spec(shape):
    nd = len(shape)
    return pl.BlockSpec(shape, lambda *_: (0,) * nd, pipeline_mode=pl.Buffered(1))


def _nt_dot(a, b):
    return lax.dot_general(a, b, (((1,), (1,)), ((), ())), preferred_element_type=F32)


def _dot(a, b):
    return jnp.dot(a, b, preferred_element_type=F32)


def _rms_rows(x, g):
    ms = jnp.mean(x * x, axis=-1, keepdims=True)
    return x * lax.rsqrt(ms + EPS) * g


def _rms_cols(xT, g):
    ms = jnp.mean(xT * xT, axis=0, keepdims=True)
    return xT * lax.rsqrt(ms + EPS) * g


def _rope_cols(xT, cos, sin, bs):
    d = xT.shape[0]
    parts = [xT[i * bs:(i + 1) * bs] for i in range(d // bs)]
    swapped = jnp.concatenate([parts[i ^ 1] for i in range(len(parts))], axis=0)
    return xT * cos + swapped * sin


def _rope_tables(seq, d):
    half = d // 2
    t = jnp.arange(seq, dtype=jnp.int32)
    row = (t // GRID_W).astype(F32)
    col = (t % GRID_W).astype(F32)
    inv = ROPE_THETA ** (-jnp.arange(0, half, 2, dtype=F32) / half)
    ang_r = (row[:, None] * inv[None, :]).T
    ang_c = (col[:, None] * inv[None, :]).T
    cos = jnp.concatenate([jnp.cos(ang_r), jnp.cos(ang_r), jnp.cos(ang_c), jnp.cos(ang_c)], axis=0)
    sin = jnp.concatenate([-jnp.sin(ang_r), jnp.sin(ang_r), -jnp.sin(ang_c), jnp.sin(ang_c)], axis=0)
    return jnp.stack([cos, sin])


def _even_proj_kernel(x_ref, gmix_ref, w1t_ref, w2_ref, gaq_ref, gak_ref, gbq_ref, gbkv_ref,
                      wuqt_ref, wk_ref, wvt_ref, ropea_ref, ropeb_ref,
                      qat_ref, ka_ref, vat_ref, qbt_ref, kb_ref, vbt_ref):
    x = x_ref[0]
    hn = _rms_rows(x, gmix_ref[...]).astype(BF16)
    zt = _nt_dot(w1t_ref[...], hn)
    z2 = _dot(hn, w2_ref[...])

    cos_a, sin_a = ropea_ref[0], ropea_ref[1]
    cos_b, sin_b = ropeb_ref[0], ropeb_ref[1]
    scale_a = HEAD_DIM ** -0.5 * LOG2E
    scale_b = (B_NOPE + B_ROPE) ** -0.5 * LOG2E

    gaq = gaq_ref[...]
    for h in range(A_HEADS):
        blk = zt[h * HEAD_DIM:(h + 1) * HEAD_DIM]
        q = _rope_cols(_rms_cols(blk, gaq), cos_a, sin_a, HEAD_DIM // 4) * scale_a
        qat_ref[0, h * HEAD_DIM:(h + 1) * HEAD_DIM, :] = q.astype(BF16)

    gak = gak_ref[...]
    kts = []
    for h in range(A_KV_HEADS):
        blk = zt[A_Q + h * HEAD_DIM:A_Q + (h + 1) * HEAD_DIM]
        kts.append(_rope_cols(_rms_cols(blk, gak), cos_a, sin_a, HEAD_DIM // 4))
    ka_ref[0] = jnp.concatenate(kts, axis=0).T.astype(BF16)

    vat_ref[0] = zt[A_Q + A_KV:A_Q + 2 * A_KV].astype(BF16)

    c0 = A_Q + 2 * A_KV
    cqn = _rms_cols(zt[c0:c0 + B_Q_RANK], gbq_ref[...]).astype(BF16)
    qbt = _dot(wuqt_ref[...], cqn)
    for h in range(B_HEADS):
        blk = qbt[h * B_QK_PAD:(h + 1) * B_QK_PAD]
        roped = _rope_cols(blk[B_NOPE:B_NOPE + B_ROPE], cos_b, sin_b, B_ROPE // 4)
        q = jnp.concatenate([blk[:B_NOPE], roped, blk[B_NOPE + B_ROPE:]], axis=0) * scale_b
        qbt_ref[0, h * B_QK_PAD:(h + 1) * B_QK_PAD, :] = q.astype(BF16)

    k0 = c0 + B_Q_RANK
    kr = _rope_cols(zt[k0:k0 + B_ROPE], cos_b, sin_b, B_ROPE // 4)
    tm = kr.shape[1]
    krp = jnp.concatenate([jnp.zeros((B_NOPE, tm), F32), kr,
                           jnp.zeros((B_QK_PAD - B_NOPE - B_ROPE, tm), F32)], axis=0).T

    kvn = _rms_rows(z2, gbkv_ref[...]).astype(BF16)
    kn = _dot(kvn, wk_ref[...])
    for h in range(B_HEADS):
        kb_ref[0, :, h * B_QK_PAD:(h + 1) * B_QK_PAD] = (
            kn[:, h * B_QK_PAD:(h + 1) * B_QK_PAD] + krp).astype(BF16)
    vbt_ref[0] = _nt_dot(wvt_ref[...], kvn).astype(BF16)


def _even_proj(x, gmix, w1t, w2, gaq, gak, gbq, gbkv, wuqt, wk, wvt, ropea, ropeb):
    b, s, d = x.shape
    tm = PROJ_TM
    tok = lambda c: pl.BlockSpec((1, c, tm), lambda i, j: (i, 0, j))
    row = lambda c: pl.BlockSpec((1, tm, c), lambda i, j: (i, j, 0))
    return pl.pallas_call(
        _even_proj_kernel,
        grid=(b, s // tm),
        in_specs=[row(d), _const_spec(gmix.shape), _const_spec(w1t.shape), _const_spec(w2.shape),
                  _const_spec(gaq.shape), _const_spec(gak.shape), _const_spec(gbq.shape),
                  _const_spec(gbkv.shape), _const_spec(wuqt.shape), _const_spec(wk.shape),
                  _const_spec(wvt.shape),
                  pl.BlockSpec((2, HEAD_DIM, tm), lambda i, j: (0, 0, j)),
                  pl.BlockSpec((2, B_ROPE, tm), lambda i, j: (0, 0, j))],
        out_specs=[tok(A_Q), row(A_KV), tok(A_KV), tok(B_HEADS * B_QK_PAD),
                   row(B_HEADS * B_QK_PAD), tok(B_HEADS * B_V)],
        out_shape=[jax.ShapeDtypeStruct((b, A_Q, s), BF16),
                   jax.ShapeDtypeStruct((b, s, A_KV), BF16),
                   jax.ShapeDtypeStruct((b, A_KV, s), BF16),
                   jax.ShapeDtypeStruct((b, B_HEADS * B_QK_PAD, s), BF16),
                   jax.ShapeDtypeStruct((b, s, B_HEADS * B_QK_PAD), BF16),
                   jax.ShapeDtypeStruct((b, B_HEADS * B_V, s), BF16)],
        compiler_params=pltpu.CompilerParams(
            dimension_semantics=("arbitrary", "arbitrary"), vmem_limit_bytes=VMEM_LIMIT),
        name="even_proj",
    )(x, gmix, w1t, w2, gaq, gak, gbq, gbkv, wuqt, wk, wvt, ropea, ropeb)


def _flash_kernel(qt_ref, k_ref, vt_ref, o_ref, qz_ref, m_ref, l_ref, acc_ref, *, gqa, tk):
    seq = k_ref.shape[1]
    tq = qt_ref.shape[2]
    m_cols = qz_ref.shape[2]
    dv = HEAD_DIM

    for j in range(2):
        if gqa:
            g = m_cols // tq
            qcat = jnp.concatenate(
                [qt_ref[0, (j * g + i) * HEAD_DIM:(j * g + i + 1) * HEAD_DIM, :] for i in range(g)], axis=1)
            zeros = jnp.zeros_like(qcat)
            qz_ref[j] = jnp.concatenate([qcat, zeros] if j == 0 else [zeros, qcat], axis=0)
        else:
            qz_ref[j] = qt_ref[0, j * B_QK_PAD:(j + 1) * B_QK_PAD, :]
    m_ref[...] = jnp.full(m_ref.shape, NEG, F32)
    l_ref[...] = jnp.zeros(l_ref.shape, F32)
    acc_ref[...] = jnp.zeros(acc_ref.shape, F32)

    def chunk(c, carry):
        off = pl.multiple_of(c * tk, tk)
        for j in range(2):
            if gqa:
                kc = k_ref[0, pl.ds(off, tk), :]
            else:
                kc = k_ref[0, pl.ds(off, tk), j * B_QK_PAD:(j + 1) * B_QK_PAD]
            s = _dot(kc, qz_ref[j])
            m_prev = m_ref[j]
            m_new = jnp.maximum(m_prev, jnp.max(s, axis=0, keepdims=True))
            alpha = jnp.exp2(m_prev - m_new)
            p = jnp.exp2(s - m_new)
            l_ref[j] = alpha * l_ref[j] + jnp.sum(p, axis=0, keepdims=True)
            vc = vt_ref[0, j * dv:(j + 1) * dv, pl.ds(off, tk)]
            acc_ref[j] = alpha * acc_ref[j] + _dot(vc, p.astype(BF16))
            m_ref[j] = m_new
        return carry

    lax.fori_loop(0, seq // tk, chunk, 0)

    for j in range(2):
        o = acc_ref[j] / l_ref[j]
        if gqa:
            g = m_cols // tq
            for i in range(g):
                o_ref[0, (j * g + i) * dv:(j * g + i + 1) * dv, :] = o[:, i * tq:(i + 1) * tq].astype(BF16)
        else:
            o_ref[0, j * dv:(j + 1) * dv, :] = o.astype(BF16)


def _flash_gqa(qt, k, vt):
    b, cq, s = qt.shape
    g = A_HEADS // A_KV_HEADS
    tq = FLASH_M // g
    return pl.pallas_call(
        functools.partial(_flash_kernel, gqa=True, tk=FLASH_TK),
        grid=(b, s // tq),
        in_specs=[pl.BlockSpec((1, cq, tq), lambda i, j: (i, 0, j)),
                  pl.BlockSpec((1, s, A_KV), lambda i, j: (i, 0, 0)),
                  pl.BlockSpec((1, A_KV, s), lambda i, j: (i, 0, 0))],
        out_specs=pl.BlockSpec((1, cq, tq), lambda i, j: (i, 0, j)),
        out_shape=jax.ShapeDtypeStruct((b, cq, s), BF16),
        scratch_shapes=[pltpu.VMEM((2, 2 * HEAD_DIM, FLASH_M), BF16),
                        pltpu.VMEM((2, 1, FLASH_M), F32),
                        pltpu.VMEM((2, 1, FLASH_M), F32),
                        pltpu.VMEM((2, HEAD_DIM, FLASH_M), F32)],
        compiler_params=pltpu.CompilerParams(
            dimension_semantics=("arbitrary", "arbitrary"), vmem_limit_bytes=VMEM_LIMIT),
        name="flash_gqa",
    )(qt, k, vt)


def _flash_mla(qt, k, vt):
    b, _, s = qt.shape
    tq = FLASH_M
    pairs = B_HEADS // 2
    return pl.pallas_call(
        functools.partial(_flash_kernel, gqa=False, tk=FLASH_TK),
        grid=(b, pairs, s // tq),
        in_specs=[pl.BlockSpec((1, 2 * B_QK_PAD, tq), lambda i, h, j: (i, h, j)),
                  pl.BlockSpec((1, s, 2 * B_QK_PAD), lambda i, h, j: (i, 0, h)),
                  pl.BlockSpec((1, 2 * B_V, s), lambda i, h, j: (i, h, 0))],
        out_specs=pl.BlockSpec((1, 2 * B_V, tq), lambda i, h, j: (i, h, j)),
        out_shape=jax.ShapeDtypeStruct((b, B_HEADS * B_V, s), BF16),
        scratch_shapes=[pltpu.VMEM((2, B_QK_PAD, FLASH_M), BF16),
                        pltpu.VMEM((2, 1, FLASH_M), F32),
                        pltpu.VMEM((2, 1, FLASH_M), F32),
                        pltpu.VMEM((2, B_V, FLASH_M), F32)],
        compiler_params=pltpu.CompilerParams(
            dimension_semantics=("arbitrary", "arbitrary", "arbitrary"), vmem_limit_bytes=VMEM_LIMIT),
        name="flash_mla",
    )(qt, k, vt)


def _mlp_kernel(*refs, n_mix, final):
    h_ref = refs[0]
    mix_refs = refs[1:1 + 2 * n_mix]
    gffn_ref, wup_ref, wdown_ref = refs[1 + 2 * n_mix:4 + 2 * n_mix]
    rest = refs[4 + 2 * n_mix:]
    gfin_ref = rest[0] if final else None
    o_ref = rest[-1]

    at = None
    for i in range(n_mix):
        part = _dot(mix_refs[2 * i][...], mix_refs[2 * i + 1][0])
        at = part if at is None else at + part
    h1 = h_ref[0] + at.T
    hn = _rms_rows(h1, gffn_ref[...]).astype(BF16)
    acc = h1
    n_chunks = D_FF // D_MODEL
    for c in range(n_chunks):
        u = jnp.maximum(_dot(hn, wup_ref[:, c * D_MODEL:(c + 1) * D_MODEL]), 0.0)
        acc = acc + _dot((u * u).astype(BF16), wdown_ref[c * D_MODEL:(c + 1) * D_MODEL, :])
    if final:
        acc = _rms_rows(acc, gfin_ref[...])
    o_ref[0] = acc


def _outproj_mlp(h, mixes, gffn, wup, wdown, gfin=None):
    b, s, d = h.shape
    tm = MLP_TM
    args = [h]
    in_specs = [pl.BlockSpec((1, tm, d), lambda i, j: (i, j, 0))]
    for wt, ot in mixes:
        args += [wt, ot]
        in_specs += [_const_spec(wt.shape), pl.BlockSpec((1, ot.shape[1], tm), lambda i, j: (i, 0, j))]
    args += [gffn, wup, wdown]
    in_specs += [_const_spec(gffn.shape), _const_spec(wup.shape), _const_spec(wdown.shape)]
    if gfin is not None:
        args.append(gfin)
        in_specs.append(_const_spec(gfin.shape))
    return pl.pallas_call(
        functools.partial(_mlp_kernel, n_mix=len(mixes), final=gfin is not None),
        grid=(b, s // tm),
        in_specs=in_specs,
        out_specs=pl.BlockSpec((1, tm, d), lambda i, j: (i, j, 0)),
        out_shape=jax.ShapeDtypeStruct((b, s, d), F32),
        compiler_params=pltpu.CompilerParams(
            dimension_semantics=("arbitrary", "arbitrary"), vmem_limit_bytes=VMEM_LIMIT),
        name="outproj_mlp",
    )(*args)


def _odd_proj_kernel(h_ref, g_ref, wqt_ref, wk_ref, wvt_ref, qt_ref, k_ref, vt_ref):
    hn = _rms_rows(h_ref[0], g_ref[...]).astype(BF16)
    scale = HEAD_DIM ** -0.5 * LOG2E
    qt_ref[0] = (_nt_dot(wqt_ref[...], hn) * scale).astype(BF16)
    k_ref[0] = _dot(hn, wk_ref[...]).astype(BF16)
    vt_ref[0] = _nt_dot(wvt_ref[...], hn).astype(BF16)


def _odd_proj(h, g, wqt, wk, wvt):
    b, s, d = h.shape
    tm = PROJ_TM
    c = wqt.shape[0]
    tok = pl.BlockSpec((1, c, tm), lambda i, j: (i, 0, j))
    row = pl.BlockSpec((1, tm, c), lambda i, j: (i, j, 0))
    return pl.pallas_call(
        _odd_proj_kernel,
        grid=(b, s // tm),
        in_specs=[pl.BlockSpec((1, tm, d), lambda i, j: (i, j, 0)), _const_spec(g.shape),
                  _const_spec(wqt.shape), _const_spec(wk.shape), _const_spec(wvt.shape)],
        out_specs=[tok, row, tok],
        out_shape=[jax.ShapeDtypeStruct((b, c, s), BF16),
                   jax.ShapeDtypeStruct((b, s, c), BF16),
                   jax.ShapeDtypeStruct((b, c, s), BF16)],
        compiler_params=pltpu.CompilerParams(
            dimension_semantics=("arbitrary", "arbitrary"), vmem_limit_bytes=VMEM_LIMIT),
        name="odd_proj",
    )(h, g, wqt, wk, wvt)


def _nat_block_geometry(rows):
    nblk = rows // NAT_QROWS
    kinds = []
    for blk in (0, 1, nblk - 1):
        r0 = blk * NAT_QROWS
        ws = min(max(r0 - C_WIN_ROWS // 2, 0), rows - NAT_KROWS)
        table = []
        for kr_l in range(NAT_KROWS):
            line = []
            for qr_l in range(NAT_QROWS):
                qr, kr = r0 + qr_l, ws + kr_l
                rs = min(max(qr - C_WIN_ROWS // 2, 0), rows - C_WIN_ROWS)
                line.append(kr - qr + C_WIN_ROWS - 1 if rs <= kr < rs + C_WIN_ROWS else None)
            table.append(line)
        kinds.append(table)
    return kinds


def _nat_kernel(qt_ref, k_ref, vt_ref, tab_ref, o_ref, bias_ref, *, rows):
    w = GRID_W
    nq = NAT_QROWS * w
    nk = NAT_KROWS * w
    nblk = rows // NAT_QROWS
    kinds = _nat_block_geometry(rows)

    @pl.when(pl.program_id(1) == 0)
    def _():
        lane = lax.broadcasted_iota(jnp.int32, (w, 2 * w), 1)
        neg = jnp.full((w, 2 * w), NEG, F32)
        for j in range(2):
            for t, table in enumerate(kinds):
                for kr_l in range(NAT_KROWS):
                    for qp in range(NAT_QROWS // 2):
                        ia, ib = table[kr_l][2 * qp], table[kr_l][2 * qp + 1]
                        ta = neg if ia is None else tab_ref[j, ia]
                        tb = neg if ib is None else tab_ref[j, ib]
                        tile = neg if (ia is None and ib is None) else jnp.where(lane < w, ta, tb)
                        bias_ref[j, t, kr_l * w:(kr_l + 1) * w, qp * 2 * w:(qp + 1) * 2 * w] = tile

    def block(q_off, k_off, kind):
        for j in range(2):
            q = qt_ref[0, j * HEAD_DIM:(j + 1) * HEAD_DIM, pl.ds(q_off, nq)]
            zeros = jnp.zeros_like(q)
            qz = jnp.concatenate([q, zeros] if j == 0 else [zeros, q], axis=0)
            kw = k_ref[0, pl.ds(k_off, nk), :]
            s = _dot(kw, qz) + bias_ref[j, kind]
            m = jnp.max(s, axis=0, keepdims=True)
            p = jnp.exp2(s - m)
            l = jnp.sum(p, axis=0, keepdims=True)
            vw = vt_ref[0, j * HEAD_DIM:(j + 1) * HEAD_DIM, pl.ds(k_off, nk)]
            o = _dot(vw, p.astype(BF16)) / l
            o_ref[0, j * HEAD_DIM:(j + 1) * HEAD_DIM, pl.ds(q_off, nq)] = o.astype(BF16)

    block(0, 0, 0)

    def interior(i, carry):
        q_off = pl.multiple_of(i * nq, nq)
        k_off = pl.multiple_of(i * nq - (C_WIN_ROWS // 2) * w, (C_WIN_ROWS // 2) * w)
        block(q_off, k_off, 1)
        return carry

    lax.fori_loop(1, nblk - 1, interior, 0)
    block((nblk - 1) * nq, (rows - NAT_KROWS) * w, 2)


def _natten(qt, k, vt, tab):
    b, c, s = qt.shape
    rows = s // GRID_W
    pairs = C_HEADS // 2
    pc = 2 * HEAD_DIM
    nrel = 2 * C_WIN_ROWS - 1
    return pl.pallas_call(
        functools.partial(_nat_kernel, rows=rows),
        grid=(pairs, b),
        in_specs=[pl.BlockSpec((1, pc, s), lambda h, i: (i, h, 0)),
                  pl.BlockSpec((1, s, pc), lambda h, i: (i, 0, h)),
                  pl.BlockSpec((1, pc, s), lambda h, i: (i, h, 0)),
                  pl.BlockSpec((2, nrel, GRID_W, 2 * GRID_W), lambda h, i: (h, 0, 0, 0))],
        out_specs=pl.BlockSpec((1, pc, s), lambda h, i: (i, h, 0)),
        out_shape=jax.ShapeDtypeStruct((b, c, s), BF16),
        scratch_shapes=[pltpu.VMEM((2, 3, NAT_KROWS * GRID_W, NAT_QROWS * GRID_W), F32)],
        compiler_params=pltpu.CompilerParams(
            dimension_semantics=("arbitrary", "arbitrary"), vmem_limit_bytes=VMEM_LIMIT),
        name="natten",
    )(qt, k, vt, tab)


def _nat_bias_tiles(rpb):
    kc = jnp.arange(GRID_W, dtype=jnp.int32)[:, None]
    qc = jnp.arange(GRID_W, dtype=jnp.int32)[None, :]
    cs = jnp.clip(qc - C_WIN_COLS // 2, 0, GRID_W - C_WIN_COLS)
    valid = (kc >= cs) & (kc < cs + C_WIN_COLS)
    rel = jnp.clip(kc - qc + (C_WIN_COLS - 1), 0, 2 * C_WIN_COLS - 2)
    tiles = jnp.where(valid[None, None], rpb[:, :, rel] * LOG2E, NEG)
    return jnp.concatenate([tiles, tiles], axis=-1).astype(F32)


def _even_layer_weights(w_in, w_uq, w_ukv):
    qa, ka, va, cq, ckv, kr = jnp.split(
        w_in, (A_Q, A_Q + A_KV, A_Q + 2 * A_KV, A_Q + 2 * A_KV + B_Q_RANK,
               A_Q + 2 * A_KV + B_Q_RANK + B_KV_RANK), axis=1)
    w1t = jnp.concatenate([qa, ka, va, cq, kr], axis=1).T.astype(BF16)
    w2 = ckv.astype(BF16)
    uq = w_uq.reshape(B_Q_RANK, B_HEADS, B_NOPE + B_ROPE)
    uq = jnp.pad(uq, ((0, 0), (0, 0), (0, B_QK_PAD - B_NOPE - B_ROPE)))
    wuqt = uq.reshape(B_Q_RANK, B_HEADS * B_QK_PAD).T.astype(BF16)
    ukv = w_ukv.reshape(B_KV_RANK, B_HEADS, B_NOPE + B_V)
    wk = jnp.pad(ukv[:, :, :B_NOPE], ((0, 0), (0, 0), (0, B_QK_PAD - B_NOPE)))
    wk = wk.reshape(B_KV_RANK, B_HEADS * B_QK_PAD).astype(BF16)
    wvt = ukv[:, :, B_NOPE:].reshape(B_KV_RANK, B_HEADS * B_V).T.astype(BF16)
    return w1t, w2, wuqt, wk, wvt


def kernel(x, norm_mix, ev_w_in, ev_a_q_norm, ev_a_k_norm, ev_b_q_norm, ev_b_w_uq, ev_b_kv_norm,
           ev_b_w_ukv, ev_w_out, od_w_qkv, od_rpb, od_w_out, norm_ffn, ffn_w_up, ffn_w_down,
           final_norm):
    depth = norm_mix.shape[0]
    s = x.shape[1]
    ropea = _rope_tables(s, HEAD_DIM)
    ropeb = _rope_tables(s, B_ROPE)
    h = x
    for layer in range(depth):
        i = layer // 2
        gmix = norm_mix[layer][None, :]
        if layer % 2 == 0:
            w1t, w2, wuqt, wk, wvt = _even_layer_weights(ev_w_in[i], ev_b_w_uq[i], ev_b_w_ukv[i])
            qat, ka, vat, qbt, kb, vbt = _even_proj(
                h, gmix, w1t, w2, ev_a_q_norm[i][:, None], ev_a_k_norm[i][:, None],
                ev_b_q_norm[i][:, None], ev_b_kv_norm[i][None, :], wuqt, wk, wvt, ropea, ropeb)
            oat = _flash_gqa(qat, ka, vat)
            obt = _flash_mla(qbt, kb, vbt)
            wot = ev_w_out[i].T.astype(BF16)
            mixes = [(wot[:, :A_Q], oat), (wot[:, A_Q:], obt)]
        else:
            wq, wk_, wv = jnp.split(od_w_qkv[i], 3, axis=1)
            qt, k, vt = _odd_proj(h, gmix, wq.T.astype(BF16), wk_.astype(BF16), wv.T.astype(BF16))
            ot = _natten(qt, k, vt, _nat_bias_tiles(od_rpb[i]))
            mixes = [(od_w_out[i].T.astype(BF16), ot)]
        last = layer == depth - 1
        h = _outproj_mlp(h, mixes, norm_ffn[layer][None, :], ffn_w_up[layer].astype(BF16),
                         ffn_w_down[layer].astype(BF16), final_norm[None, :] if last else None)
    return h
```

```python
import functools
import math

import jax
import jax.numpy as jnp
from jax import lax
from jax.experimental import pallas as pl
from jax.experimental.pallas import tpu as pltpu

D_MODEL = 1024
GRID_W = 64
HEAD_DIM = 64
ROPE_THETA = 10000.0
EPS = 1e-6
A_HEADS = 8
A_KV_HEADS = 2
B_HEADS = 8
B_Q_RANK = 384
B_KV_RANK = 256
B_NOPE = 64
B_ROPE = 32
B_V = 64
C_HEADS = 16
C_WIN_ROWS = 8
C_WIN_COLS = 16
D_FF = 4 * D_MODEL

A_Q = A_HEADS * HEAD_DIM
A_KV = A_KV_HEADS * HEAD_DIM
B_QK_PAD = 128
LOG2E = math.log2(math.e)
NEG = -1e30

BF16 = jnp.bfloat16
F32 = jnp.float32

VMEM_LIMIT = 56 * 1024 * 1024

PROJ_TM = 512
MLP_TM = 512
FLASH_TK = 512
FLASH_M = 1024
FLASH_ACC_ROWS = HEAD_DIM + 16
NAT_QROWS = 4
NAT_KROWS = 12
NAT_ACC_ROWS = HEAD_DIM + 16


def _const_spec(shape):
    nd = len(shape)
    return pl.BlockSpec(shape, lambda *_: (0,) * nd, pipeline_mode=pl.Buffered(1))


def _nt_dot(a, b):
    return lax.dot_general(a, b, (((1,), (1,)), ((), ())), preferred_element_type=F32)


def _dot(a, b):
    return jnp.dot(a, b, preferred_element_type=F32)


def _rms_rows(x, g):
    ms = jnp.mean(x * x, axis=-1, keepdims=True)
    return x * lax.rsqrt(ms + EPS) * g


def _rms_cols(xT, g):
    ms = jnp.mean(xT * xT, axis=0, keepdims=True)
    return xT * lax.rsqrt(ms + EPS) * g


def _rope_cols(xT, cos, sin, bs):
    d = xT.shape[0]
    parts = [xT[i * bs:(i + 1) * bs] for i in range(d // bs)]
    swapped = jnp.concatenate([parts[i ^ 1] for i in range(len(parts))], axis=0)
    return xT * cos + swapped * sin


def _rope_tables(seq, d):
    half = d // 2
    t = jnp.arange(seq, dtype=jnp.int32)
    row = (t // GRID_W).astype(F32)
    col = (t % GRID_W).astype(F32)
    inv = ROPE_THETA ** (-jnp.arange(0, half, 2, dtype=F32) / half)
    ang_r = (row[:, None] * inv[None, :]).T
    ang_c = (col[:, None] * inv[None, :]).T
    cos = jnp.concatenate([jnp.cos(ang_r), jnp.cos(ang_r), jnp.cos(ang_c), jnp.cos(ang_c)], axis=0)
    sin = jnp.concatenate([-jnp.sin(ang_r), jnp.sin(ang_r), -jnp.sin(ang_c), jnp.sin(ang_c)], axis=0)
    return jnp.stack([cos, sin])


def _even_proj_kernel(x_ref, gmix_ref, w1t_ref, w2_ref, gaq_ref, gak_ref, gbq_ref, gbkv_ref,
                      wuqt_ref, wk_ref, wvt_ref, ropea_ref, ropeb_ref,
                      qat_ref, ka_ref, vat_ref, qbt_ref, kb_ref, vbt_ref):
    x = x_ref[0]
    hn = _rms_rows(x, gmix_ref[...]).astype(BF16)
    zt = _nt_dot(w1t_ref[...], hn)
    z2 = _dot(hn, w2_ref[...])

    cos_a, sin_a = ropea_ref[0], ropea_ref[1]
    cos_b, sin_b = ropeb_ref[0], ropeb_ref[1]
    scale_a = HEAD_DIM ** -0.5 * LOG2E
    scale_b = (B_NOPE + B_ROPE) ** -0.5 * LOG2E

    gaq = gaq_ref[...]
    for h in range(A_HEADS):
        blk = zt[h * HEAD_DIM:(h + 1) * HEAD_DIM]
        q = _rope_cols(_rms_cols(blk, gaq), cos_a, sin_a, HEAD_DIM // 4) * scale_a
        qat_ref[0, h * HEAD_DIM:(h + 1) * HEAD_DIM, :] = q.astype(BF16)

    gak = gak_ref[...]
    kts = []
    for h in range(A_KV_HEADS):
        blk = zt[A_Q + h * HEAD_DIM:A_Q + (h + 1) * HEAD_DIM]
        kts.append(_rope_cols(_rms_cols(blk, gak), cos_a, sin_a, HEAD_DIM // 4))
    ka_ref[0] = jnp.concatenate(kts, axis=0).T.astype(BF16)

    vat_ref[0] = zt[A_Q + A_KV:A_Q + 2 * A_KV].astype(BF16)

    c0 = A_Q + 2 * A_KV
    cqn = _rms_cols(zt[c0:c0 + B_Q_RANK], gbq_ref[...]).astype(BF16)
    qbt = _dot(wuqt_ref[...], cqn)
    for h in range(B_HEADS):
        blk = qbt[h * B_QK_PAD:(h + 1) * B_QK_PAD]
        roped = _rope_cols(blk[B_NOPE:B_NOPE + B_ROPE], cos_b, sin_b, B_ROPE // 4)
        q = jnp.concatenate([blk[:B_NOPE], roped, blk[B_NOPE + B_ROPE:]], axis=0) * scale_b
        qbt_ref[0, h * B_QK_PAD:(h + 1) * B_QK_PAD, :] = q.astype(BF16)

    k0 = c0 + B_Q_RANK
    kr = _rope_cols(zt[k0:k0 + B_ROPE], cos_b, sin_b, B_ROPE // 4)
    tm = kr.shape[1]
    krp = jnp.concatenate([jnp.zeros((B_NOPE, tm), F32), kr,
                           jnp.zeros((B_QK_PAD - B_NOPE - B_ROPE, tm), F32)], axis=0).T

    kvn = _rms_rows(z2, gbkv_ref[...]).astype(BF16)
    kn = _dot(kvn, wk_ref[...])
    for h in range(B_HEADS):
        kb_ref[0, :, h * B_QK_PAD:(h + 1) * B_QK_PAD] = (
            kn[:, h * B_QK_PAD:(h + 1) * B_QK_PAD] + krp).astype(BF16)
    vbt_ref[0] = _nt_dot(wvt_ref[...], kvn).astype(BF16)


def _even_proj(x, gmix, w1t, w2, gaq, gak, gbq, gbkv, wuqt, wk, wvt, ropea, ropeb):
    b, s, d = x.shape
    tm = PROJ_TM
    tok = lambda c: pl.BlockSpec((1, c, tm), lambda i, j: (i, 0, j))
    row = lambda c: pl.BlockSpec((1, tm, c), lambda i, j: (i, j, 0))
    return pl.pallas_call(
        _even_proj_kernel,
        grid=(b, s // tm),
        in_specs=[row(d), _const_spec(gmix.shape), _const_spec(w1t.shape), _const_spec(w2.shape),
                  _const_spec(gaq.shape), _const_spec(gak.shape), _const_spec(gbq.shape),
                  _const_spec(gbkv.shape), _const_spec(wuqt.shape), _const_spec(wk.shape),
                  _const_spec(wvt.shape),
                  pl.BlockSpec((2, HEAD_DIM, tm), lambda i, j: (0, 0, j)),
                  pl.BlockSpec((2, B_ROPE, tm), lambda i, j: (0, 0, j))],
        out_specs=[tok(A_Q), row(A_KV), tok(A_KV), tok(B_HEADS * B_QK_PAD),
                   row(B_HEADS * B_QK_PAD), tok(B_HEADS * B_V)],
        out_shape=[jax.ShapeDtypeStruct((b, A_Q, s), BF16),
                   jax.ShapeDtypeStruct((b, s, A_KV), BF16),
                   jax.ShapeDtypeStruct((b, A_KV, s), BF16),
                   jax.ShapeDtypeStruct((b, B_HEADS * B_QK_PAD, s), BF16),
                   jax.ShapeDtypeStruct((b, s, B_HEADS * B_QK_PAD), BF16),
                   jax.ShapeDtypeStruct((b, B_HEADS * B_V, s), BF16)],
        compiler_params=pltpu.CompilerParams(
            dimension_semantics=("arbitrary", "arbitrary"), vmem_limit_bytes=VMEM_LIMIT),
        name="even_proj",
    )(x, gmix, w1t, w2, gaq, gak, gbq, gbkv, wuqt, wk, wvt, ropea, ropeb)


def _flash_kernel(qt_ref, k_ref, vt_ref, o_ref, qz_ref, m_ref, acc_ref,
                  sa_ref, sb_ref, mxa_ref, mxb_ref, *, gqa, tk):
    seq = k_ref.shape[1]
    tq = qt_ref.shape[2]
    m_cols = qz_ref.shape[2]
    dv = HEAD_DIM
    n = seq // tk

    for j in range(2):
        if gqa:
            g = m_cols // tq
            qcat = jnp.concatenate(
                [qt_ref[0, (j * g + i) * HEAD_DIM:(j * g + i + 1) * HEAD_DIM, :] for i in range(g)], axis=1)
            zeros = jnp.zeros_like(qcat)
            qz_ref[j] = jnp.concatenate([qcat, zeros] if j == 0 else [zeros, qcat], axis=0)
        else:
            qz_ref[j] = qt_ref[0, j * B_QK_PAD:(j + 1) * B_QK_PAD, :]
    m_ref[...] = jnp.full(m_ref.shape, NEG, F32)
    acc_ref[...] = jnp.zeros(acc_ref.shape, F32)

    def scores(c, s_ref, mx_ref):
        off = pl.multiple_of(c * tk, tk)
        for j in range(2):
            if gqa:
                kc = k_ref[0, pl.ds(off, tk), :]
            else:
                kc = k_ref[0, pl.ds(off, tk), j * B_QK_PAD:(j + 1) * B_QK_PAD]
            s = _dot(kc, qz_ref[j])
            s_ref[j] = s
            mx_ref[j] = jnp.max(s, axis=0, keepdims=True)

    ones = jnp.ones((FLASH_ACC_ROWS - dv, tk), BF16)

    def softmax_pv(c, s_ref, mx_ref):
        off = pl.multiple_of(c * tk, tk)
        for j in range(2):
            m_prev = m_ref[j]
            m_new = jnp.maximum(m_prev, mx_ref[j])
            alpha = jnp.exp2(m_prev - m_new)
            p = jnp.exp2(s_ref[j] - m_new).astype(BF16)
            vc = jnp.concatenate([vt_ref[0, j * dv:(j + 1) * dv, pl.ds(off, tk)], ones], axis=0)
            acc_ref[j] = alpha * acc_ref[j] + _dot(vc, p)
            m_ref[j] = m_new

    scores(0, sa_ref, mxa_ref)

    def pair(i, carry):
        c = 2 * i
        scores(c + 1, sb_ref, mxb_ref)
        softmax_pv(c, sa_ref, mxa_ref)
        scores(c + 2, sa_ref, mxa_ref)
        softmax_pv(c + 1, sb_ref, mxb_ref)
        return carry

    lax.fori_loop(0, n // 2 - 1, pair, 0)
    scores(n - 1, sb_ref, mxb_ref)
    softmax_pv(n - 2, sa_ref, mxa_ref)
    softmax_pv(n - 1, sb_ref, mxb_ref)

    for j in range(2):
        acc = acc_ref[j]
        o = acc[:dv] / acc[dv:dv + 1]
        if gqa:
            g = m_cols // tq
            for i in range(g):
                o_ref[0, (j * g + i) * dv:(j * g + i + 1) * dv, :] = o[:, i * tq:(i + 1) * tq].astype(BF16)
        else:
            o_ref[0, j * dv:(j + 1) * dv, :] = o.astype(BF16)


def _flash_scratch(dk):
    return [pltpu.VMEM((2, dk, FLASH_M), BF16),
            pltpu.VMEM((2, 1, FLASH_M), F32),
            pltpu.VMEM((2, FLASH_ACC_ROWS, FLASH_M), F32),
            pltpu.VMEM((2, FLASH_TK, FLASH_M), F32),
            pltpu.VMEM((2, FLASH_TK, FLASH_M), F32),
            pltpu.VMEM((2, 1, FLASH_M), F32),
            pltpu.VMEM((2, 1, FLASH_M), F32)]


def _flash_gqa(qt, k, vt):
    b, cq, s = qt.shape
    g = A_HEADS // A_KV_HEADS
    tq = FLASH_M // g
    return pl.pallas_call(
        functools.partial(_flash_kernel, gqa=True, tk=FLASH_TK),
        grid=(b, s // tq),
        in_specs=[pl.BlockSpec((1, cq, tq), lambda i, j: (i, 0, j)),
                  pl.BlockSpec((1, s, A_KV), lambda i, j: (i, 0, 0)),
                  pl.BlockSpec((1, A_KV, s), lambda i, j: (i, 0, 0))],
        out_specs=pl.BlockSpec((1, cq, tq), lambda i, j: (i, 0, j)),
        out_shape=jax.ShapeDtypeStruct((b, cq, s), BF16),
        scratch_shapes=_flash_scratch(2 * HEAD_DIM),
        compiler_params=pltpu.CompilerParams(
            dimension_semantics=("arbitrary", "arbitrary"), vmem_limit_bytes=VMEM_LIMIT),
        name="flash_gqa",
    )(qt, k, vt)


def _flash_mla(qt, k, vt):
    b, _, s = qt.shape
    tq = FLASH_M
    pairs = B_HEADS // 2
    return pl.pallas_call(
        functools.partial(_flash_kernel, gqa=False, tk=FLASH_TK),
        grid=(b, pairs, s // tq),
        in_specs=[pl.BlockSpec((1, 2 * B_QK_PAD, tq), lambda i, h, j: (i, h, j)),
                  pl.BlockSpec((1, s, 2 * B_QK_PAD), lambda i, h, j: (i, 0, h)),
                  pl.BlockSpec((1, 2 * B_V, s), lambda i, h, j: (i, h, 0))],
        out_specs=pl.BlockSpec((1, 2 * B_V, tq), lambda i, h, j: (i, h, j)),
        out_shape=jax.ShapeDtypeStruct((b, B_HEADS * B_V, s), BF16),
        scratch_shapes=_flash_scratch(B_QK_PAD),
        compiler_params=pltpu.CompilerParams(
            dimension_semantics=("arbitrary", "arbitrary", "arbitrary"), vmem_limit_bytes=VMEM_LIMIT),
        name="flash_mla",
    )(qt, k, vt)


def _mlp_kernel(*refs, n_mix, final):
    h_ref = refs[0]
    mix_refs = refs[1:1 + 2 * n_mix]
    gffn_ref, wup_ref, wdown_ref = refs[1 + 2 * n_mix:4 + 2 * n_mix]
    rest = refs[4 + 2 * n_mix:]
    gfin_ref = rest[0] if final else None
    o_ref = rest[-1]

    at = None
    for i in range(n_mix):
        part = _dot(mix_refs[2 * i][...], mix_refs[2 * i + 1][0])
        at = part if at is None else at + part
    h1 = h_ref[0] + at.T
    hn = _rms_rows(h1, gffn_ref[...]).astype(BF16)
    acc = h1
    n_chunks = D_FF // D_MODEL
    for c in range(n_chunks):
        u = jnp.maximum(_dot(hn, wup_ref[:, c * D_MODEL:(c + 1) * D_MODEL]), 0.0)
        acc = acc + _dot((u * u).astype(BF16), wdown_ref[c * D_MODEL:(c + 1) * D_MODEL, :])
    if final:
        acc = _rms_rows(acc, gfin_ref[...])
    o_ref[0] = acc


def _outproj_mlp(h, mixes, gffn, wup, wdown, gfin=None):
    b, s, d = h.shape
    tm = MLP_TM
    args = [h]
    in_specs = [pl.BlockSpec((1, tm, d), lambda i, j: (i, j, 0))]
    for wt, ot in mixes:
        args += [wt, ot]
        in_specs += [_const_spec(wt.shape), pl.BlockSpec((1, ot.shape[1], tm), lambda i, j: (i, 0, j))]
    args += [gffn, wup, wdown]
    in_specs += [_const_spec(gffn.shape), _const_spec(wup.shape), _const_spec(wdown.shape)]
    if gfin is not None:
        args.append(gfin)
        in_specs.append(_const_spec(gfin.shape))
    return pl.pallas_call(
        functools.partial(_mlp_kernel, n_mix=len(mixes), final=gfin is not None),
        grid=(b, s // tm),
        in_specs=in_specs,
        out_specs=pl.BlockSpec((1, tm, d), lambda i, j: (i, j, 0)),
        out_shape=jax.ShapeDtypeStruct((b, s, d), F32),
        compiler_params=pltpu.CompilerParams(
            dimension_semantics=("arbitrary", "arbitrary"), vmem_limit_bytes=VMEM_LIMIT),
        name="outproj_mlp",
    )(*args)


def _odd_proj_kernel(h_ref, g_ref, wqt_ref, wk_ref, wvt_ref, qt_ref, k_ref, vt_ref):
    hn = _rms_rows(h_ref[0], g_ref[...]).astype(BF16)
    scale = HEAD_DIM ** -0.5 * LOG2E
    qt_ref[0] = (_nt_dot(wqt_ref[...], hn) * scale).astype(BF16)
    k_ref[0] = _dot(hn, wk_ref[...]).astype(BF16)
    vt_ref[0] = _nt_dot(wvt_ref[...], hn).astype(BF16)


def _odd_proj(h, g, wqt, wk, wvt):
    b, s, d = h.shape
    tm = PROJ_TM
    c = wqt.shape[0]
    tok = pl.BlockSpec((1, c, tm), lambda i, j: (i, 0, j))
    row = pl.BlockSpec((1, tm, c), lambda i, j: (i, j, 0))
    return pl.pallas_call(
        _odd_proj_kernel,
        grid=(b, s // tm),
        in_specs=[pl.BlockSpec((1, tm, d), lambda i, j: (i, j, 0)), _const_spec(g.shape),
                  _const_spec(wqt.shape), _const_spec(wk.shape), _const_spec(wvt.shape)],
        out_specs=[tok, row, tok],
        out_shape=[jax.ShapeDtypeStruct((b, c, s), BF16),
                   jax.ShapeDtypeStruct((b, s, c), BF16),
                   jax.ShapeDtypeStruct((b, c, s), BF16)],
        compiler_params=pltpu.CompilerParams(
            dimension_semantics=("arbitrary", "arbitrary"), vmem_limit_bytes=VMEM_LIMIT),
        name="odd_proj",
    )(h, g, wqt, wk, wvt)


def _nat_block_geometry(rows):
    nblk = rows // NAT_QROWS
    kinds = []
    for blk in (0, 1, nblk - 1):
        r0 = blk * NAT_QROWS
        ws = min(max(r0 - C_WIN_ROWS // 2, 0), rows - NAT_KROWS)
        table = []
        for kr_l in range(NAT_KROWS):
            line = []
            for qr_l in range(NAT_QROWS):
                qr, kr = r0 + qr_l, ws + kr_l
                rs = min(max(qr - C_WIN_ROWS // 2, 0), rows - C_WIN_ROWS)
                line.append(kr - qr + C_WIN_ROWS - 1 if rs <= kr < rs + C_WIN_ROWS else None)
            table.append(line)
        kinds.append(table)
    return kinds


def _nat_kernel(qt_ref, k_ref, vt_ref, tab_ref, o_ref, bias_ref, sa_ref, sb_ref, mxa_ref, mxb_ref,
                *, rows):
    w = GRID_W
    nq = NAT_QROWS * w
    nk = NAT_KROWS * w
    nblk = rows // NAT_QROWS
    assert nblk >= 4 and nblk % 2 == 0 and rows >= NAT_KROWS
    kinds = _nat_block_geometry(rows)

    @pl.when(pl.program_id(1) == 0)
    def _():
        lane = lax.broadcasted_iota(jnp.int32, (w, 2 * w), 1)
        neg = jnp.full((w, 2 * w), NEG, F32)
        for j in range(2):
            for t, table in enumerate(kinds):
                for kr_l in range(NAT_KROWS):
                    for qp in range(NAT_QROWS // 2):
                        ia, ib = table[kr_l][2 * qp], table[kr_l][2 * qp + 1]
                        ta = neg if ia is None else tab_ref[j, ia]
                        tb = neg if ib is None else tab_ref[j, ib]
                        tile = neg if (ia is None and ib is None) else jnp.where(lane < w, ta, tb)
                        bias_ref[j, t, kr_l * w:(kr_l + 1) * w, qp * 2 * w:(qp + 1) * 2 * w] = tile

    def scores(offs, kind, s_ref, mx_ref):
        q_off, k_off = offs
        kw = k_ref[0, pl.ds(k_off, nk), :]
        for j in range(2):
            q = qt_ref[0, j * HEAD_DIM:(j + 1) * HEAD_DIM, pl.ds(q_off, nq)]
            zeros = jnp.zeros_like(q)
            qz = jnp.concatenate([q, zeros] if j == 0 else [zeros, q], axis=0)
            s = _dot(kw, qz) + bias_ref[j, kind]
            s_ref[j] = s
            mx_ref[j] = jnp.max(s, axis=0, keepdims=True)

    ones = jnp.ones((NAT_ACC_ROWS - HEAD_DIM, nk), BF16)

    def softmax_pv(offs, s_ref, mx_ref):
        q_off, k_off = offs
        for j in range(2):
            p = jnp.exp2(s_ref[j] - mx_ref[j]).astype(BF16)
            vw = jnp.concatenate(
                [vt_ref[0, j * HEAD_DIM:(j + 1) * HEAD_DIM, pl.ds(k_off, nk)], ones], axis=0)
            pv = _dot(vw, p)
            o = pv[:HEAD_DIM] / pv[HEAD_DIM:HEAD_DIM + 1]
            o_ref[0, j * HEAD_DIM:(j + 1) * HEAD_DIM, pl.ds(q_off, nq)] = o.astype(BF16)

    half = (C_WIN_ROWS // 2) * w

    def interior(i):
        if isinstance(i, int):
            return i * nq, i * nq - half
        return pl.multiple_of(i * nq, nq), pl.multiple_of(i * nq - half, half)

    first = (0, 0)
    last = ((nblk - 1) * nq, (rows - NAT_KROWS) * w)

    scores(first, 0, sa_ref, mxa_ref)
    scores(interior(1), 1, sb_ref, mxb_ref)
    softmax_pv(first, sa_ref, mxa_ref)

    def pair(t, carry):
        c = 2 + 2 * t
        scores(interior(c), 1, sa_ref, mxa_ref)
        softmax_pv(interior(c - 1), sb_ref, mxb_ref)
        scores(interior(c + 1), 1, sb_ref, mxb_ref)
        softmax_pv(interior(c), sa_ref, mxa_ref)
        return carry

    lax.fori_loop(0, (nblk - 4) // 2, pair, 0)
    scores(interior(nblk - 2), 1, sa_ref, mxa_ref)
    softmax_pv(interior(nblk - 3), sb_ref, mxb_ref)
    scores(last, 2, sb_ref, mxb_ref)
    softmax_pv(interior(nblk - 2), sa_ref, mxa_ref)
    softmax_pv(last, sb_ref, mxb_ref)


def _natten(qt, k, vt, tab):
    b, c, s = qt.shape
    rows = s // GRID_W
    pairs = C_HEADS // 2
    pc = 2 * HEAD_DIM
    nrel = 2 * C_WIN_ROWS - 1
    return pl.pallas_call(
        functools.partial(_nat_kernel, rows=rows),
        grid=(pairs, b),
        in_specs=[pl.BlockSpec((1, pc, s), lambda h, i: (i, h, 0)),
                  pl.BlockSpec((1, s, pc), lambda h, i: (i, 0, h)),
                  pl.BlockSpec((1, pc, s), lambda h, i: (i, h, 0)),
                  pl.BlockSpec((2, nrel, GRID_W, 2 * GRID_W), lambda h, i: (h, 0, 0, 0))],
        out_specs=pl.BlockSpec((1, pc, s), lambda h, i: (i, h, 0)),
        out_shape=jax.ShapeDtypeStruct((b, c, s), BF16),
        scratch_shapes=[pltpu.VMEM((2, 3, NAT_KROWS * GRID_W, NAT_QROWS * GRID_W), F32),
                        pltpu.VMEM((2, NAT_KROWS * GRID_W, NAT_QROWS * GRID_W), F32),
                        pltpu.VMEM((2, NAT_KROWS * GRID_W, NAT_QROWS * GRID_W), F32),
                        pltpu.VMEM((2, 1, NAT_QROWS * GRID_W), F32),
                        pltpu.VMEM((2, 1, NAT_QROWS * GRID_W), F32)],
        compiler_params=pltpu.CompilerParams(
            dimension_semantics=("arbitrary", "arbitrary"), vmem_limit_bytes=VMEM_LIMIT),
        name="natten",
    )(qt, k, vt, tab)


def _nat_bias_tiles(rpb):
    kc = jnp.arange(GRID_W, dtype=jnp.int32)[:, None]
    qc = jnp.arange(GRID_W, dtype=jnp.int32)[None, :]
    cs = jnp.clip(qc - C_WIN_COLS // 2, 0, GRID_W - C_WIN_COLS)
    valid = (kc >= cs) & (kc < cs + C_WIN_COLS)
    rel = jnp.clip(kc - qc + (C_WIN_COLS - 1), 0, 2 * C_WIN_COLS - 2)
    tiles = jnp.where(valid[None, None], rpb[:, :, rel] * LOG2E, NEG)
    return jnp.concatenate([tiles, tiles], axis=-1).astype(F32)


def _even_layer_weights(w_in, w_uq, w_ukv):
    qa, ka, va, cq, ckv, kr = jnp.split(
        w_in, (A_Q, A_Q + A_KV, A_Q + 2 * A_KV, A_Q + 2 * A_KV + B_Q_RANK,
               A_Q + 2 * A_KV + B_Q_RANK + B_KV_RANK), axis=1)
    w1t = jnp.concatenate([qa, ka, va, cq, kr], axis=1).T.astype(BF16)
    w2 = ckv.astype(BF16)
    uq = w_uq.reshape(B_Q_RANK, B_HEADS, B_NOPE + B_ROPE)
    uq = jnp.pad(uq, ((0, 0), (0, 0), (0, B_QK_PAD - B_NOPE - B_ROPE)))
    wuqt = uq.reshape(B_Q_RANK, B_HEADS * B_QK_PAD).T.astype(BF16)
    ukv = w_ukv.reshape(B_KV_RANK, B_HEADS, B_NOPE + B_V)
    wk = jnp.pad(ukv[:, :, :B_NOPE], ((0, 0), (0, 0), (0, B_QK_PAD - B_NOPE)))
    wk = wk.reshape(B_KV_RANK, B_HEADS * B_QK_PAD).astype(BF16)
    wvt = ukv[:, :, B_NOPE:].reshape(B_KV_RANK, B_HEADS * B_V).T.astype(BF16)
    return w1t, w2, wuqt, wk, wvt


def kernel(x, norm_mix, ev_w_in, ev_a_q_norm, ev_a_k_norm, ev_b_q_norm, ev_b_w_uq, ev_b_kv_norm,
           ev_b_w_ukv, ev_w_out, od_w_qkv, od_rpb, od_w_out, norm_ffn, ffn_w_up, ffn_w_down,
           final_norm):
    depth = norm_mix.shape[0]
    s = x.shape[1]
    ropea = _rope_tables(s, HEAD_DIM)
    ropeb = _rope_tables(s, B_ROPE)
    h = x
    for layer in range(depth):
        i = layer // 2
        gmix = norm_mix[layer][None, :]
        if layer % 2 == 0:
            w1t, w2, wuqt, wk, wvt = _even_layer_weights(ev_w_in[i], ev_b_w_uq[i], ev_b_w_ukv[i])
            qat, ka, vat, qbt, kb, vbt = _even_proj(
                h, gmix, w1t, w2, ev_a_q_norm[i][:, None], ev_a_k_norm[i][:, None],
                ev_b_q_norm[i][:, None], ev_b_kv_norm[i][None, :], wuqt, wk, wvt, ropea, ropeb)
            oat = _flash_gqa(qat, ka, vat)
            obt = _flash_mla(qbt, kb, vbt)
            wot = ev_w_out[i].T.astype(BF16)
            mixes = [(wot[:, :A_Q], oat), (wot[:, A_Q:], obt)]
        else:
            wq, wk_, wv = jnp.split(od_w_qkv[i], 3, axis=1)
            qt, k, vt = _odd_proj(h, gmix, wq.T.astype(BF16), wk_.astype(BF16), wv.T.astype(BF16))
            ot = _natten(qt, k, vt, _nat_bias_tiles(od_rpb[i]))
            mixes = [(od_w_out[i].T.astype(BF16), ot)]
        last = layer == depth - 1
        h = _outproj_mlp(h, mixes, norm_ffn[layer][None, :], ffn_w_up[layer].astype(BF16),
                         ffn_w_down[layer].astype(BF16), final_norm[None, :] if last else None)
    return h
```

```python
import functools
import math

import numpy as np

import jax
import jax.numpy as jnp
from jax import lax
from jax.experimental import pallas as pl
from jax.experimental.pallas import tpu as pltpu

D_MODEL = 1024
GRID_W = 64
HEAD_DIM = 64
ROPE_THETA = 10000.0
EPS = 1e-6
A_HEADS = 8
A_KV_HEADS = 2
B_HEADS = 8
B_Q_RANK = 384
B_KV_RANK = 256
B_NOPE = 64
B_ROPE = 32
B_V = 64
C_HEADS = 16
C_WIN_ROWS = 8
C_WIN_COLS = 16
D_FF = 4 * D_MODEL

A_Q = A_HEADS * HEAD_DIM
A_KV = A_KV_HEADS * HEAD_DIM
B_QK_PAD = 128
LOG2E = math.log2(math.e)
NEG = -1e30

BF16 = jnp.bfloat16
F32 = jnp.float32

VMEM_LIMIT = 56 * 1024 * 1024

PROJ_TM = 512
MLP_TM = 512
FLASH_TK = 512
FLASH_M = 1024
FLASH_SLAB = 256
FLASH_UNROLL = 4
FLASH_ACC_ROWS = HEAD_DIM + 16
NAT_QROWS = 4
NAT_KROWS = 12
NAT_UNROLL = 4
NAT_ACC_ROWS = HEAD_DIM + 16


def _const_spec(shape):
    nd = len(shape)
    return pl.BlockSpec(shape, lambda *_: (0,) * nd, pipeline_mode=pl.Buffered(1))


def _nt_dot(a, b):
    return lax.dot_general(a, b, (((1,), (1,)), ((), ())), preferred_element_type=F32)


def _dot(a, b):
    return jnp.dot(a, b, preferred_element_type=F32)


def _rms_rows(x, g):
    ms = jnp.mean(x * x, axis=-1, keepdims=True)
    return x * lax.rsqrt(ms + EPS) * g


def _rms_cols(xT, g):
    ms = jnp.mean(xT * xT, axis=0, keepdims=True)
    return xT * lax.rsqrt(ms + EPS) * g


def _rope_cols(xT, cos, sin, bs):
    d = xT.shape[0]
    parts = [xT[i * bs:(i + 1) * bs] for i in range(d // bs)]
    swapped = jnp.concatenate([parts[i ^ 1] for i in range(len(parts))], axis=0)
    return xT * cos + swapped * sin


def _rope_tables(seq, d):
    half = d // 2
    t = jnp.arange(seq, dtype=jnp.int32)
    row = (t // GRID_W).astype(F32)
    col = (t % GRID_W).astype(F32)
    inv = ROPE_THETA ** (-jnp.arange(0, half, 2, dtype=F32) / half)
    ang_r = (row[:, None] * inv[None, :]).T
    ang_c = (col[:, None] * inv[None, :]).T
    cos = jnp.concatenate([jnp.cos(ang_r), jnp.cos(ang_r), jnp.cos(ang_c), jnp.cos(ang_c)], axis=0)
    sin = jnp.concatenate([-jnp.sin(ang_r), jnp.sin(ang_r), -jnp.sin(ang_c), jnp.sin(ang_c)], axis=0)
    return jnp.stack([cos, sin])


def _even_proj_kernel(x_ref, gmix_ref, w1t_ref, w2_ref, gaq_ref, gak_ref, gbq_ref, gbkv_ref,
                      wuqt_ref, wk_ref, wvt_ref, ropea_ref, ropeb_ref,
                      qat_ref, ka_ref, vat_ref, qbt_ref, kb_ref, vbt_ref):
    x = x_ref[0]
    hn = _rms_rows(x, gmix_ref[...]).astype(BF16)
    zt = _nt_dot(w1t_ref[...], hn)
    z2 = _dot(hn, w2_ref[...])

    cos_a, sin_a = ropea_ref[0], ropea_ref[1]
    cos_b, sin_b = ropeb_ref[0], ropeb_ref[1]
    scale_a = HEAD_DIM ** -0.5 * LOG2E
    scale_b = (B_NOPE + B_ROPE) ** -0.5 * LOG2E

    gaq = gaq_ref[...]
    for h in range(A_HEADS):
        blk = zt[h * HEAD_DIM:(h + 1) * HEAD_DIM]
        q = _rope_cols(_rms_cols(blk, gaq), cos_a, sin_a, HEAD_DIM // 4) * scale_a
        qat_ref[0, h * HEAD_DIM:(h + 1) * HEAD_DIM, :] = q.astype(BF16)

    gak = gak_ref[...]
    kts = []
    for h in range(A_KV_HEADS):
        blk = zt[A_Q + h * HEAD_DIM:A_Q + (h + 1) * HEAD_DIM]
        kts.append(_rope_cols(_rms_cols(blk, gak), cos_a, sin_a, HEAD_DIM // 4))
    ka_ref[0] = jnp.concatenate(kts, axis=0).T.astype(BF16)

    vat_ref[0] = zt[A_Q + A_KV:A_Q + 2 * A_KV].astype(BF16)

    c0 = A_Q + 2 * A_KV
    cqn = _rms_cols(zt[c0:c0 + B_Q_RANK], gbq_ref[...]).astype(BF16)
    qbt = _dot(wuqt_ref[...], cqn)
    for h in range(B_HEADS):
        blk = qbt[h * B_QK_PAD:(h + 1) * B_QK_PAD]
        roped = _rope_cols(blk[B_NOPE:B_NOPE + B_ROPE], cos_b, sin_b, B_ROPE // 4)
        q = jnp.concatenate([blk[:B_NOPE], roped, blk[B_NOPE + B_ROPE:]], axis=0) * scale_b
        qbt_ref[0, h * B_QK_PAD:(h + 1) * B_QK_PAD, :] = q.astype(BF16)

    k0 = c0 + B_Q_RANK
    kr = _rope_cols(zt[k0:k0 + B_ROPE], cos_b, sin_b, B_ROPE // 4)
    tm = kr.shape[1]
    krp = jnp.concatenate([jnp.zeros((B_NOPE, tm), F32), kr,
                           jnp.zeros((B_QK_PAD - B_NOPE - B_ROPE, tm), F32)], axis=0).T

    kvn = _rms_rows(z2, gbkv_ref[...]).astype(BF16)
    kn = _dot(kvn, wk_ref[...])
    for h in range(B_HEADS):
        kb_ref[0, :, h * B_QK_PAD:(h + 1) * B_QK_PAD] = (
            kn[:, h * B_QK_PAD:(h + 1) * B_QK_PAD] + krp).astype(BF16)
    vbt_ref[0] = _nt_dot(wvt_ref[...], kvn).astype(BF16)


def _even_proj(x, gmix, w1t, w2, gaq, gak, gbq, gbkv, wuqt, wk, wvt, ropea, ropeb):
    b, s, d = x.shape
    tm = PROJ_TM
    tok = lambda c: pl.BlockSpec((1, c, tm), lambda i, j: (i, 0, j))
    row = lambda c: pl.BlockSpec((1, tm, c), lambda i, j: (i, j, 0))
    return pl.pallas_call(
        _even_proj_kernel,
        grid=(b, s // tm),
        in_specs=[row(d), _const_spec(gmix.shape), _const_spec(w1t.shape), _const_spec(w2.shape),
                  _const_spec(gaq.shape), _const_spec(gak.shape), _const_spec(gbq.shape),
                  _const_spec(gbkv.shape), _const_spec(wuqt.shape), _const_spec(wk.shape),
                  _const_spec(wvt.shape),
                  pl.BlockSpec((2, HEAD_DIM, tm), lambda i, j: (0, 0, j)),
                  pl.BlockSpec((2, B_ROPE, tm), lambda i, j: (0, 0, j))],
        out_specs=[tok(A_Q), row(A_KV), tok(A_KV), tok(B_HEADS * B_QK_PAD),
                   row(B_HEADS * B_QK_PAD), tok(B_HEADS * B_V)],
        out_shape=[jax.ShapeDtypeStruct((b, A_Q, s), BF16),
                   jax.ShapeDtypeStruct((b, s, A_KV), BF16),
                   jax.ShapeDtypeStruct((b, A_KV, s), BF16),
                   jax.ShapeDtypeStruct((b, B_HEADS * B_QK_PAD, s), BF16),
                   jax.ShapeDtypeStruct((b, s, B_HEADS * B_QK_PAD), BF16),
                   jax.ShapeDtypeStruct((b, B_HEADS * B_V, s), BF16)],
        compiler_params=pltpu.CompilerParams(
            dimension_semantics=("arbitrary", "arbitrary"), vmem_limit_bytes=VMEM_LIMIT),
        name="even_proj",
    )(x, gmix, w1t, w2, gaq, gak, gbq, gbkv, wuqt, wk, wvt, ropea, ropeb)


def _flash_kernel(qt_ref, k_ref, vt_ref, o_ref, qz_ref, m_ref, acc_ref,
                  sa_ref, sb_ref, mxa_ref, mxb_ref, *, gqa, tk):
    seq = k_ref.shape[1]
    tq = qt_ref.shape[2]
    m_cols = qz_ref.shape[2]
    dv = HEAD_DIM
    n = seq // tk

    for j in range(2):
        if gqa:
            g = m_cols // tq
            qcat = jnp.concatenate(
                [qt_ref[0, (j * g + i) * HEAD_DIM:(j * g + i + 1) * HEAD_DIM, :] for i in range(g)], axis=1)
            zeros = jnp.zeros_like(qcat)
            qz_ref[j] = jnp.concatenate([qcat, zeros] if j == 0 else [zeros, qcat], axis=0)
        else:
            qz_ref[j] = qt_ref[0, j * B_QK_PAD:(j + 1) * B_QK_PAD, :]
    m_ref[...] = jnp.full(m_ref.shape, NEG, F32)
    acc_ref[...] = jnp.zeros(acc_ref.shape, F32)

    slabs = [(j, sl * FLASH_SLAB) for j in range(2) for sl in range(m_cols // FLASH_SLAB)]

    def k_chunk(c, j):
        off = pl.multiple_of(c * tk, tk)
        if gqa:
            return k_ref[0, pl.ds(off, tk), :]
        return k_ref[0, pl.ds(off, tk), j * B_QK_PAD:(j + 1) * B_QK_PAD]

    ones = jnp.ones((FLASH_ACC_ROWS - dv, tk), BF16)

    def v_chunk(c, j):
        off = pl.multiple_of(c * tk, tk)
        return jnp.concatenate([vt_ref[0, j * dv:(j + 1) * dv, pl.ds(off, tk)], ones], axis=0)

    def scores(kc, j, col, s_ref, mx_ref):
        cols = slice(col, col + FLASH_SLAB)
        s = _dot(kc, qz_ref[j, :, cols])
        s_ref[j, :, cols] = s
        mx_ref[j, :, cols] = jnp.max(s, axis=0, keepdims=True).astype(F32)

    def softmax_pv(vc, j, col, s_ref, mx_ref):
        cols = slice(col, col + FLASH_SLAB)
        m_prev = m_ref[j, :, cols]
        m_new = jnp.maximum(m_prev, mx_ref[j, :, cols])
        alpha = jnp.exp2(m_prev - m_new)
        p = jnp.exp2(s_ref[j, :, cols] - m_new).astype(BF16)
        acc_ref[j, :, cols] = alpha * acc_ref[j, :, cols] + _dot(vc, p)
        m_ref[j, :, cols] = m_new

    def step(c_next, nxt, c_cur, cur):
        kcs = [k_chunk(c_next, j) for j in range(2)] if nxt is not None else None
        vcs = [v_chunk(c_cur, j) for j in range(2)] if cur is not None else None
        for j, col in slabs:
            if nxt is not None:
                scores(kcs[j], j, col, *nxt)
            if cur is not None:
                softmax_pv(vcs[j], j, col, *cur)

    bufs = ((sa_ref, mxa_ref), (sb_ref, mxb_ref))
    step(0, bufs[0], None, None)
    iters = (n - 1) // FLASH_UNROLL

    def body(i, carry):
        c0 = 1 + FLASH_UNROLL * i
        for u in range(FLASH_UNROLL):
            step(c0 + u, bufs[(1 + u) % 2], c0 + u - 1, bufs[u % 2])
        return carry

    if iters:
        lax.fori_loop(0, iters, body, 0)
    for c in range(1 + FLASH_UNROLL * iters, n):
        step(c, bufs[c % 2], c - 1, bufs[(c - 1) % 2])
    step(None, None, n - 1, bufs[(n - 1) % 2])

    for j in range(2):
        acc = acc_ref[j]
        o = acc[:dv] / acc[dv:dv + 1]
        if gqa:
            g = m_cols // tq
            for i in range(g):
                o_ref[0, (j * g + i) * dv:(j * g + i + 1) * dv, :] = o[:, i * tq:(i + 1) * tq].astype(BF16)
        else:
            o_ref[0, j * dv:(j + 1) * dv, :] = o.astype(BF16)


def _flash_scratch(dk):
    return [pltpu.VMEM((2, dk, FLASH_M), BF16),
            pltpu.VMEM((2, 1, FLASH_M), F32),
            pltpu.VMEM((2, FLASH_ACC_ROWS, FLASH_M), F32),
            pltpu.VMEM((2, FLASH_TK, FLASH_M), F32),
            pltpu.VMEM((2, FLASH_TK, FLASH_M), F32),
            pltpu.VMEM((2, 1, FLASH_M), F32),
            pltpu.VMEM((2, 1, FLASH_M), F32)]


def _flash_gqa(qt, k, vt):
    b, cq, s = qt.shape
    g = A_HEADS // A_KV_HEADS
    tq = FLASH_M // g
    return pl.pallas_call(
        functools.partial(_flash_kernel, gqa=True, tk=FLASH_TK),
        grid=(b, s // tq),
        in_specs=[pl.BlockSpec((1, cq, tq), lambda i, j: (i, 0, j)),
                  pl.BlockSpec((1, s, A_KV), lambda i, j: (i, 0, 0)),
                  pl.BlockSpec((1, A_KV, s), lambda i, j: (i, 0, 0))],
        out_specs=pl.BlockSpec((1, cq, tq), lambda i, j: (i, 0, j)),
        out_shape=jax.ShapeDtypeStruct((b, cq, s), BF16),
        scratch_shapes=_flash_scratch(2 * HEAD_DIM),
        compiler_params=pltpu.CompilerParams(
            dimension_semantics=("arbitrary", "arbitrary"), vmem_limit_bytes=VMEM_LIMIT),
        name="flash_gqa",
    )(qt, k, vt)


def _flash_mla(qt, k, vt):
    b, _, s = qt.shape
    tq = FLASH_M
    pairs = B_HEADS // 2
    return pl.pallas_call(
        functools.partial(_flash_kernel, gqa=False, tk=FLASH_TK),
        grid=(b, pairs, s // tq),
        in_specs=[pl.BlockSpec((1, 2 * B_QK_PAD, tq), lambda i, h, j: (i, h, j)),
                  pl.BlockSpec((1, s, 2 * B_QK_PAD), lambda i, h, j: (i, 0, h)),
                  pl.BlockSpec((1, 2 * B_V, s), lambda i, h, j: (i, h, 0))],
        out_specs=pl.BlockSpec((1, 2 * B_V, tq), lambda i, h, j: (i, h, j)),
        out_shape=jax.ShapeDtypeStruct((b, B_HEADS * B_V, s), BF16),
        scratch_shapes=_flash_scratch(B_QK_PAD),
        compiler_params=pltpu.CompilerParams(
            dimension_semantics=("arbitrary", "arbitrary", "arbitrary"), vmem_limit_bytes=VMEM_LIMIT),
        name="flash_mla",
    )(qt, k, vt)


def _mlp_kernel(*refs, n_mix, final):
    h_ref = refs[0]
    mix_refs = refs[1:1 + 2 * n_mix]
    gffn_ref, wup_ref, wdown_ref = refs[1 + 2 * n_mix:4 + 2 * n_mix]
    rest = refs[4 + 2 * n_mix:]
    gfin_ref = rest[0] if final else None
    o_ref = rest[-1]

    at = None
    for i in range(n_mix):
        part = _dot(mix_refs[2 * i][...], mix_refs[2 * i + 1][0])
        at = part if at is None else at + part
    h1 = h_ref[0] + at.T
    hn = _rms_rows(h1, gffn_ref[...]).astype(BF16)
    acc = h1
    n_chunks = D_FF // D_MODEL
    for c in range(n_chunks):
        u = jnp.maximum(_dot(hn, wup_ref[:, c * D_MODEL:(c + 1) * D_MODEL]), 0.0)
        acc = acc + _dot((u * u).astype(BF16), wdown_ref[c * D_MODEL:(c + 1) * D_MODEL, :])
    if final:
        acc = _rms_rows(acc, gfin_ref[...])
    o_ref[0] = acc


def _outproj_mlp(h, mixes, gffn, wup, wdown, gfin=None):
    b, s, d = h.shape
    tm = MLP_TM
    args = [h]
    in_specs = [pl.BlockSpec((1, tm, d), lambda i, j: (i, j, 0))]
    for wt, ot in mixes:
        args += [wt, ot]
        in_specs += [_const_spec(wt.shape), pl.BlockSpec((1, ot.shape[1], tm), lambda i, j: (i, 0, j))]
    args += [gffn, wup, wdown]
    in_specs += [_const_spec(gffn.shape), _const_spec(wup.shape), _const_spec(wdown.shape)]
    if gfin is not None:
        args.append(gfin)
        in_specs.append(_const_spec(gfin.shape))
    return pl.pallas_call(
        functools.partial(_mlp_kernel, n_mix=len(mixes), final=gfin is not None),
        grid=(b, s // tm),
        in_specs=in_specs,
        out_specs=pl.BlockSpec((1, tm, d), lambda i, j: (i, j, 0)),
        out_shape=jax.ShapeDtypeStruct((b, s, d), F32),
        compiler_params=pltpu.CompilerParams(
            dimension_semantics=("arbitrary", "arbitrary"), vmem_limit_bytes=VMEM_LIMIT),
        name="outproj_mlp",
    )(*args)


def _odd_proj_kernel(h_ref, g_ref, wqt_ref, wk_ref, wvt_ref, qt_ref, k_ref, vt_ref):
    hn = _rms_rows(h_ref[0], g_ref[...]).astype(BF16)
    scale = HEAD_DIM ** -0.5 * LOG2E
    qt_ref[0] = (_nt_dot(wqt_ref[...], hn) * scale).astype(BF16)
    k_ref[0] = _dot(hn, wk_ref[...]).astype(BF16)
    vt_ref[0] = _nt_dot(wvt_ref[...], hn).astype(BF16)


def _odd_proj(h, g, wqt, wk, wvt):
    b, s, d = h.shape
    tm = PROJ_TM
    c = wqt.shape[0]
    tok = pl.BlockSpec((1, c, tm), lambda i, j: (i, 0, j))
    row = pl.BlockSpec((1, tm, c), lambda i, j: (i, j, 0))
    return pl.pallas_call(
        _odd_proj_kernel,
        grid=(b, s // tm),
        in_specs=[pl.BlockSpec((1, tm, d), lambda i, j: (i, j, 0)), _const_spec(g.shape),
                  _const_spec(wqt.shape), _const_spec(wk.shape), _const_spec(wvt.shape)],
        out_specs=[tok, row, tok],
        out_shape=[jax.ShapeDtypeStruct((b, c, s), BF16),
                   jax.ShapeDtypeStruct((b, s, c), BF16),
                   jax.ShapeDtypeStruct((b, c, s), BF16)],
        compiler_params=pltpu.CompilerParams(
            dimension_semantics=("arbitrary", "arbitrary"), vmem_limit_bytes=VMEM_LIMIT),
        name="odd_proj",
    )(h, g, wqt, wk, wvt)


def _nat_block_geometry(rows):
    nblk = rows // NAT_QROWS
    kinds = []
    for blk in (0, 1, nblk - 1):
        r0 = blk * NAT_QROWS
        ws = min(max(r0 - C_WIN_ROWS // 2, 0), rows - NAT_KROWS)
        table = []
        for kr_l in range(NAT_KROWS):
            line = []
            for qr_l in range(NAT_QROWS):
                qr, kr = r0 + qr_l, ws + kr_l
                rs = min(max(qr - C_WIN_ROWS // 2, 0), rows - C_WIN_ROWS)
                line.append(kr - qr + C_WIN_ROWS - 1 if rs <= kr < rs + C_WIN_ROWS else None)
            table.append(line)
        kinds.append(table)
    return kinds


def _nat_kernel(qt_ref, k_ref, vt_ref, tab_ref, o_ref, bias_ref, sa_ref, sb_ref, mxa_ref, mxb_ref,
                *, rows):
    w = GRID_W
    nq = NAT_QROWS * w
    nk = NAT_KROWS * w
    nblk = rows // NAT_QROWS
    assert nblk >= 4 and nblk % 2 == 0 and rows >= NAT_KROWS
    kinds = _nat_block_geometry(rows)

    @pl.when(pl.program_id(1) == 0)
    def _():
        lane = lax.broadcasted_iota(jnp.int32, (w, 2 * w), 1)
        neg = jnp.full((w, 2 * w), NEG, F32)
        for j in range(2):
            for t, table in enumerate(kinds):
                for kr_l in range(NAT_KROWS):
                    for qp in range(NAT_QROWS // 2):
                        ia, ib = table[kr_l][2 * qp], table[kr_l][2 * qp + 1]
                        ta = neg if ia is None else tab_ref[j, ia]
                        tb = neg if ib is None else tab_ref[j, ib]
                        tile = neg if (ia is None and ib is None) else jnp.where(lane < w, ta, tb)
                        bias_ref[j, t, kr_l * w:(kr_l + 1) * w, qp * 2 * w:(qp + 1) * 2 * w] = tile

    def scores(offs, kind, j, s_ref, mx_ref):
        q_off, k_off = offs
        kw = k_ref[0, pl.ds(k_off, nk), :]
        q = qt_ref[0, j * HEAD_DIM:(j + 1) * HEAD_DIM, pl.ds(q_off, nq)]
        zeros = jnp.zeros_like(q)
        qz = jnp.concatenate([q, zeros] if j == 0 else [zeros, q], axis=0)
        s = _dot(kw, qz) + bias_ref[j, kind]
        s_ref[j] = s
        mx_ref[j] = jnp.max(s, axis=0, keepdims=True)

    ones = jnp.ones((NAT_ACC_ROWS - HEAD_DIM, nk), BF16)

    def softmax_pv(offs, j, s_ref, mx_ref):
        q_off, k_off = offs
        p = jnp.exp2(s_ref[j] - mx_ref[j]).astype(BF16)
        vw = jnp.concatenate(
            [vt_ref[0, j * HEAD_DIM:(j + 1) * HEAD_DIM, pl.ds(k_off, nk)], ones], axis=0)
        pv = _dot(vw, p)
        o = pv[:HEAD_DIM] / pv[HEAD_DIM:HEAD_DIM + 1]
        o_ref[0, j * HEAD_DIM:(j + 1) * HEAD_DIM, pl.ds(q_off, nq)] = o.astype(BF16)

    half = (C_WIN_ROWS // 2) * w

    def desc(t):
        if isinstance(t, int):
            if t == 0:
                return (0, 0), 0
            if t == nblk - 1:
                return ((nblk - 1) * nq, (rows - NAT_KROWS) * w), 2
            return (t * nq, t * nq - half), 1
        return (pl.multiple_of(t * nq, nq), pl.multiple_of(t * nq - half, half)), 1

    bufs = ((sa_ref, mxa_ref), (sb_ref, mxb_ref))

    def step(t_next, par_next, t_cur):
        for j in range(2):
            if t_next is not None:
                offs, kind = desc(t_next)
                scores(offs, kind, j, *bufs[par_next])
            if t_cur is not None:
                softmax_pv(desc(t_cur)[0], j, *bufs[1 - par_next])

    step(0, 0, None)
    step(1, 1, 0)
    iters = (nblk - 3) // NAT_UNROLL

    def body(i, carry):
        t0 = 2 + NAT_UNROLL * i
        for u in range(NAT_UNROLL):
            step(t0 + u, u % 2, t0 + u - 1)
        return carry

    if iters:
        lax.fori_loop(0, iters, body, 0)
    for t in range(2 + NAT_UNROLL * iters, nblk):
        step(t, t % 2, t - 1)
    step(None, nblk % 2, nblk - 1)


def _natten(qt, k, vt, tab):
    b, c, s = qt.shape
    rows = s // GRID_W
    pairs = C_HEADS // 2
    pc = 2 * HEAD_DIM
    nrel = 2 * C_WIN_ROWS - 1
    return pl.pallas_call(
        functools.partial(_nat_kernel, rows=rows),
        grid=(pairs, b),
        in_specs=[pl.BlockSpec((1, pc, s), lambda h, i: (i, h, 0)),
                  pl.BlockSpec((1, s, pc), lambda h, i: (i, 0, h)),
                  pl.BlockSpec((1, pc, s), lambda h, i: (i, h, 0)),
                  pl.BlockSpec((2, nrel, GRID_W, 2 * GRID_W), lambda h, i: (h, 0, 0, 0))],
        out_specs=pl.BlockSpec((1, pc, s), lambda h, i: (i, h, 0)),
        out_shape=jax.ShapeDtypeStruct((b, c, s), BF16),
        scratch_shapes=[pltpu.VMEM((2, 3, NAT_KROWS * GRID_W, NAT_QROWS * GRID_W), F32),
                        pltpu.VMEM((2, NAT_KROWS * GRID_W, NAT_QROWS * GRID_W), F32),
                        pltpu.VMEM((2, NAT_KROWS * GRID_W, NAT_QROWS * GRID_W), F32),
                        pltpu.VMEM((2, 1, NAT_QROWS * GRID_W), F32),
                        pltpu.VMEM((2, 1, NAT_QROWS * GRID_W), F32)],
        compiler_params=pltpu.CompilerParams(
            dimension_semantics=("arbitrary", "arbitrary"), vmem_limit_bytes=VMEM_LIMIT),
        name="natten",
    )(qt, k, vt, tab)


def _nat_bias_tiles(rpb):
    kc = np.arange(GRID_W)[:, None]
    qc = np.arange(GRID_W)[None, :]
    cs = np.clip(qc - C_WIN_COLS // 2, 0, GRID_W - C_WIN_COLS)
    valid = (kc >= cs) & (kc < cs + C_WIN_COLS)
    rel = kc - qc + (C_WIN_COLS - 1)
    onehot = (rel[None] == np.arange(2 * C_WIN_COLS - 1)[:, None, None]).astype(np.float32)
    tiles = jnp.einsum("had,dkq->hakq", rpb * LOG2E, onehot, precision=lax.Precision.HIGHEST)
    tiles = jnp.where(valid[None, None], tiles, NEG)
    return jnp.concatenate([tiles, tiles], axis=-1).astype(F32)


def _even_layer_weights(w_in, w_uq, w_ukv):
    qa, ka, va, cq, ckv, kr = jnp.split(
        w_in, (A_Q, A_Q + A_KV, A_Q + 2 * A_KV, A_Q + 2 * A_KV + B_Q_RANK,
               A_Q + 2 * A_KV + B_Q_RANK + B_KV_RANK), axis=1)
    w1t = jnp.concatenate([qa, ka, va, cq, kr], axis=1).T.astype(BF16)
    w2 = ckv.astype(BF16)
    uq = w_uq.reshape(B_Q_RANK, B_HEADS, B_NOPE + B_ROPE)
    uq = jnp.pad(uq, ((0, 0), (0, 0), (0, B_QK_PAD - B_NOPE - B_ROPE)))
    wuqt = uq.reshape(B_Q_RANK, B_HEADS * B_QK_PAD).T.astype(BF16)
    ukv = w_ukv.reshape(B_KV_RANK, B_HEADS, B_NOPE + B_V)
    wk = jnp.pad(ukv[:, :, :B_NOPE], ((0, 0), (0, 0), (0, B_QK_PAD - B_NOPE)))
    wk = wk.reshape(B_KV_RANK, B_HEADS * B_QK_PAD).astype(BF16)
    wvt = ukv[:, :, B_NOPE:].reshape(B_KV_RANK, B_HEADS * B_V).T.astype(BF16)
    return w1t, w2, wuqt, wk, wvt


def kernel(x, norm_mix, ev_w_in, ev_a_q_norm, ev_a_k_norm, ev_b_q_norm, ev_b_w_uq, ev_b_kv_norm,
           ev_b_w_ukv, ev_w_out, od_w_qkv, od_rpb, od_w_out, norm_ffn, ffn_w_up, ffn_w_down,
           final_norm):
    depth = norm_mix.shape[0]
    s = x.shape[1]
    ropea = _rope_tables(s, HEAD_DIM)
    ropeb = _rope_tables(s, B_ROPE)
    h = x
    for layer in range(depth):
        i = layer // 2
        gmix = norm_mix[layer][None, :]
        if layer % 2 == 0:
            w1t, w2, wuqt, wk, wvt = _even_layer_weights(ev_w_in[i], ev_b_w_uq[i], ev_b_w_ukv[i])
            qat, ka, vat, qbt, kb, vbt = _even_proj(
                h, gmix, w1t, w2, ev_a_q_norm[i][:, None], ev_a_k_norm[i][:, None],
                ev_b_q_norm[i][:, None], ev_b_kv_norm[i][None, :], wuqt, wk, wvt, ropea, ropeb)
            oat = _flash_gqa(qat, ka, vat)
            obt = _flash_mla(qbt, kb, vbt)
            wot = ev_w_out[i].T.astype(BF16)
            mixes = [(wot[:, :A_Q], oat), (wot[:, A_Q:], obt)]
        else:
            wq, wk_, wv = jnp.split(od_w_qkv[i], 3, axis=1)
            qt, k, vt = _odd_proj(h, gmix, wq.T.astype(BF16), wk_.astype(BF16), wv.T.astype(BF16))
            ot = _natten(qt, k, vt, _nat_bias_tiles(od_rpb[i]))
            mixes = [(od_w_out[i].T.astype(BF16), ot)]
        last = layer == depth - 1
        h = _outproj_mlp(h, mixes, norm_ffn[layer][None, :], ffn_w_up[layer].astype(BF16),
                         ffn_w_down[layer].astype(BF16), final_norm[None, :] if last else None)
    return h
```

```python
import functools
import math

import numpy as np

import jax
import jax.numpy as jnp
from jax import lax
from jax.experimental import pallas as pl
from jax.experimental.pallas import tpu as pltpu

D_MODEL = 1024
GRID_W = 64
HEAD_DIM = 64
ROPE_THETA = 10000.0
EPS = 1e-6
A_HEADS = 8
A_KV_HEADS = 2
B_HEADS = 8
B_Q_RANK = 384
B_KV_RANK = 256
B_NOPE = 64
B_ROPE = 32
B_V = 64
C_HEADS = 16
C_WIN_ROWS = 8
C_WIN_COLS = 16
D_FF = 4 * D_MODEL

A_Q = A_HEADS * HEAD_DIM
A_KV = A_KV_HEADS * HEAD_DIM
B_QK_PAD = 128
LOG2E = math.log2(math.e)
NEG = -1e30

BF16 = jnp.bfloat16
F32 = jnp.float32

VMEM_LIMIT = 56 * 1024 * 1024

PROJ_TM = 512
MLP_TM = 512
FLASH_TK = 512
FLASH_M = 1024
FLASH_SLAB = 256
FLASH_UNROLL = 4
FLASH_ACC_ROWS = HEAD_DIM + 16
NAT_QROWS = 4
NAT_KROWS = 12
NAT_UNROLL = 4
NAT_ACC_ROWS = HEAD_DIM + 16


def _const_spec(shape):
    nd = len(shape)
    return pl.BlockSpec(shape, lambda *_: (0,) * nd, pipeline_mode=pl.Buffered(1))


def _nt_dot(a, b):
    return lax.dot_general(a, b, (((1,), (1,)), ((), ())), preferred_element_type=F32)


def _dot(a, b):
    return jnp.dot(a, b, preferred_element_type=F32)


def _rms_rows(x, g):
    ms = jnp.mean(x * x, axis=-1, keepdims=True)
    return x * lax.rsqrt(ms + EPS) * g


def _rms_cols(xT, g):
    ms = jnp.mean(xT * xT, axis=0, keepdims=True)
    return xT * lax.rsqrt(ms + EPS) * g


def _rope_cols(xT, cos, sin, bs):
    d = xT.shape[0]
    parts = [xT[i * bs:(i + 1) * bs] for i in range(d // bs)]
    swapped = jnp.concatenate([parts[i ^ 1] for i in range(len(parts))], axis=0)
    return xT * cos + swapped * sin


def _rope_tables(seq, d):
    half = d // 2
    t = jnp.arange(seq, dtype=jnp.int32)
    row = (t // GRID_W).astype(F32)
    col = (t % GRID_W).astype(F32)
    inv = ROPE_THETA ** (-jnp.arange(0, half, 2, dtype=F32) / half)
    ang_r = (row[:, None] * inv[None, :]).T
    ang_c = (col[:, None] * inv[None, :]).T
    cos = jnp.concatenate([jnp.cos(ang_r), jnp.cos(ang_r), jnp.cos(ang_c), jnp.cos(ang_c)], axis=0)
    sin = jnp.concatenate([-jnp.sin(ang_r), jnp.sin(ang_r), -jnp.sin(ang_c), jnp.sin(ang_c)], axis=0)
    return jnp.stack([cos, sin])


def _even_proj_kernel(x_ref, gmix_ref, w1t_ref, w2_ref, gaq_ref, gak_ref, gbq_ref, gbkv_ref,
                      wuqt_ref, wk_ref, wvt_ref, ropea_ref, ropeb_ref,
                      qat_ref, ka_ref, vat_ref, qbt_ref, kb_ref, vbt_ref):
    x = x_ref[0]
    hn = _rms_rows(x, gmix_ref[...]).astype(BF16)
    zt = _nt_dot(w1t_ref[...], hn)
    z2 = _dot(hn, w2_ref[...])

    cos_a, sin_a = ropea_ref[0], ropea_ref[1]
    cos_b, sin_b = ropeb_ref[0], ropeb_ref[1]
    scale_a = HEAD_DIM ** -0.5 * LOG2E
    scale_b = (B_NOPE + B_ROPE) ** -0.5 * LOG2E

    gaq = gaq_ref[...]
    for h in range(A_HEADS):
        blk = zt[h * HEAD_DIM:(h + 1) * HEAD_DIM]
        q = _rope_cols(_rms_cols(blk, gaq), cos_a, sin_a, HEAD_DIM // 4) * scale_a
        qat_ref[0, h * HEAD_DIM:(h + 1) * HEAD_DIM, :] = q.astype(BF16)

    gak = gak_ref[...]
    kts = []
    for h in range(A_KV_HEADS):
        blk = zt[A_Q + h * HEAD_DIM:A_Q + (h + 1) * HEAD_DIM]
        kts.append(_rope_cols(_rms_cols(blk, gak), cos_a, sin_a, HEAD_DIM // 4))
    ka_ref[0] = jnp.concatenate(kts, axis=0).T.astype(BF16)

    vat_ref[0] = zt[A_Q + A_KV:A_Q + 2 * A_KV].astype(BF16)

    c0 = A_Q + 2 * A_KV
    cqn = _rms_cols(zt[c0:c0 + B_Q_RANK], gbq_ref[...]).astype(BF16)
    qbt = _dot(wuqt_ref[...], cqn)
    for h in range(B_HEADS):
        blk = qbt[h * B_QK_PAD:(h + 1) * B_QK_PAD]
        roped = _rope_cols(blk[B_NOPE:B_NOPE + B_ROPE], cos_b, sin_b, B_ROPE // 4)
        q = jnp.concatenate([blk[:B_NOPE], roped, blk[B_NOPE + B_ROPE:]], axis=0) * scale_b
        qbt_ref[0, h * B_QK_PAD:(h + 1) * B_QK_PAD, :] = q.astype(BF16)

    k0 = c0 + B_Q_RANK
    kr = _rope_cols(zt[k0:k0 + B_ROPE], cos_b, sin_b, B_ROPE // 4)
    tm = kr.shape[1]
    krp = jnp.concatenate([jnp.zeros((B_NOPE, tm), F32), kr,
                           jnp.zeros((B_QK_PAD - B_NOPE - B_ROPE, tm), F32)], axis=0).T

    kvn = _rms_rows(z2, gbkv_ref[...]).astype(BF16)
    kn = _dot(kvn, wk_ref[...])
    for h in range(B_HEADS):
        kb_ref[0, :, h * B_QK_PAD:(h + 1) * B_QK_PAD] = (
            kn[:, h * B_QK_PAD:(h + 1) * B_QK_PAD] + krp).astype(BF16)
    vbt_ref[0] = _nt_dot(wvt_ref[...], kvn).astype(BF16)


def _even_proj(x, gmix, w1t, w2, gaq, gak, gbq, gbkv, wuqt, wk, wvt, ropea, ropeb):
    b, s, d = x.shape
    tm = PROJ_TM
    tok = lambda c: pl.BlockSpec((1, c, tm), lambda i, j: (i, 0, j))
    row = lambda c: pl.BlockSpec((1, tm, c), lambda i, j: (i, j, 0))
    return pl.pallas_call(
        _even_proj_kernel,
        grid=(b, s // tm),
        in_specs=[row(d), _const_spec(gmix.shape), _const_spec(w1t.shape), _const_spec(w2.shape),
                  _const_spec(gaq.shape), _const_spec(gak.shape), _const_spec(gbq.shape),
                  _const_spec(gbkv.shape), _const_spec(wuqt.shape), _const_spec(wk.shape),
                  _const_spec(wvt.shape),
                  pl.BlockSpec((2, HEAD_DIM, tm), lambda i, j: (0, 0, j)),
                  pl.BlockSpec((2, B_ROPE, tm), lambda i, j: (0, 0, j))],
        out_specs=[tok(A_Q), row(A_KV), tok(A_KV), tok(B_HEADS * B_QK_PAD),
                   row(B_HEADS * B_QK_PAD), tok(B_HEADS * B_V)],
        out_shape=[jax.ShapeDtypeStruct((b, A_Q, s), BF16),
                   jax.ShapeDtypeStruct((b, s, A_KV), BF16),
                   jax.ShapeDtypeStruct((b, A_KV, s), BF16),
                   jax.ShapeDtypeStruct((b, B_HEADS * B_QK_PAD, s), BF16),
                   jax.ShapeDtypeStruct((b, s, B_HEADS * B_QK_PAD), BF16),
                   jax.ShapeDtypeStruct((b, B_HEADS * B_V, s), BF16)],
        compiler_params=pltpu.CompilerParams(
            dimension_semantics=("arbitrary", "arbitrary"), vmem_limit_bytes=VMEM_LIMIT),
        name="even_proj",
    )(x, gmix, w1t, w2, gaq, gak, gbq, gbkv, wuqt, wk, wvt, ropea, ropeb)


def _flash_kernel(qt_ref, qtn_ref, k_ref, vt_ref, o_ref, qz_ref, m_ref, acc_ref,
                  sa_ref, sb_ref, mxa_ref, mxb_ref, *, gqa, tk, q_axis):
    seq = k_ref.shape[1]
    tq = qt_ref.shape[2]
    m_cols = qz_ref.shape[3]
    dv = HEAD_DIM
    n = seq // tk
    assert n % 2 == 0
    jq = pl.program_id(q_axis)
    slot = jq % 2

    def stage(src_ref, dst_slot):
        for st in range(2):
            if gqa:
                g = m_cols // tq
                qcat = jnp.concatenate(
                    [src_ref[0, (st * g + i) * HEAD_DIM:(st * g + i + 1) * HEAD_DIM, :] for i in range(g)],
                    axis=1)
                zeros = jnp.zeros_like(qcat)
                qz_ref[dst_slot, st] = jnp.concatenate([qcat, zeros] if st == 0 else [zeros, qcat], axis=0)
            else:
                qz_ref[dst_slot, st] = src_ref[0, st * B_QK_PAD:(st + 1) * B_QK_PAD, :]

    slabs = [(st, sl * FLASH_SLAB) for st in range(2) for sl in range(m_cols // FLASH_SLAB)]

    def k_chunk(c, st):
        off = pl.multiple_of(c * tk, tk)
        if gqa:
            return k_ref[0, pl.ds(off, tk), :]
        return k_ref[0, pl.ds(off, tk), st * B_QK_PAD:(st + 1) * B_QK_PAD]

    ones = jnp.ones((FLASH_ACC_ROWS - dv, tk), BF16)

    def v_chunk(c, st):
        off = pl.multiple_of(c * tk, tk)
        return jnp.concatenate([vt_ref[0, st * dv:(st + 1) * dv, pl.ds(off, tk)], ones], axis=0)

    def scores(kc, q_slot, st, col, s_ref, mx_ref):
        cols = slice(col, col + FLASH_SLAB)
        s = _dot(kc, qz_ref[q_slot, st, :, cols])
        s_ref[st, col // FLASH_SLAB] = s
        mx_ref[st, :, cols] = jnp.max(s, axis=0, keepdims=True)

    def softmax_pv(vc, st, col, s_ref, mx_ref):
        cols = slice(col, col + FLASH_SLAB)
        m_prev = m_ref[st, :, cols]
        m_new = jnp.maximum(m_prev, mx_ref[st, :, cols])
        alpha = jnp.exp2(m_prev - m_new)
        p = jnp.exp2(s_ref[st, col // FLASH_SLAB] - m_new).astype(BF16)
        acc_ref[st, :, cols] = alpha * acc_ref[st, :, cols] + _dot(vc, p)
        m_ref[st, :, cols] = m_new

    def step(c_next, q_slot, nxt, c_cur, cur):
        kcs = [k_chunk(c_next, st) for st in range(2)] if nxt is not None else None
        vcs = [v_chunk(c_cur, st) for st in range(2)] if cur is not None else None
        for st, col in slabs:
            if nxt is not None:
                scores(kcs[st], q_slot, st, col, *nxt)
            if cur is not None:
                softmax_pv(vcs[st], st, col, *cur)

    bufs = ((sa_ref, mxa_ref), (sb_ref, mxb_ref))

    @pl.when(jq == 0)
    def _():
        stage(qt_ref, 0)
        step(0, 0, bufs[0], None, None)

    stage(qtn_ref, 1 - slot)
    m_ref[...] = jnp.full(m_ref.shape, NEG, F32)
    acc_ref[...] = jnp.zeros(acc_ref.shape, F32)
    iters = (n - 1) // FLASH_UNROLL

    def body(i, carry):
        c0 = 1 + FLASH_UNROLL * i
        for u in range(FLASH_UNROLL):
            step(c0 + u, slot, bufs[(1 + u) % 2], c0 + u - 1, bufs[u % 2])
        return carry

    if iters:
        lax.fori_loop(0, iters, body, 0)
    for c in range(1 + FLASH_UNROLL * iters, n):
        step(c, slot, bufs[c % 2], c - 1, bufs[(c - 1) % 2])
    step(0, 1 - slot, bufs[0], n - 1, bufs[(n - 1) % 2])

    for st in range(2):
        acc = acc_ref[st]
        o = acc[:dv] / acc[dv:dv + 1]
        if gqa:
            g = m_cols // tq
            for i in range(g):
                o_ref[0, (st * g + i) * dv:(st * g + i + 1) * dv, :] = o[:, i * tq:(i + 1) * tq].astype(BF16)
        else:
            o_ref[0, st * dv:(st + 1) * dv, :] = o.astype(BF16)


def _flash_scratch(dk):
    return [pltpu.VMEM((2, 2, dk, FLASH_M), BF16),
            pltpu.VMEM((2, 1, FLASH_M), F32),
            pltpu.VMEM((2, FLASH_ACC_ROWS, FLASH_M), F32),
            pltpu.VMEM((2, FLASH_M // FLASH_SLAB, FLASH_TK, FLASH_SLAB), F32),
            pltpu.VMEM((2, FLASH_M // FLASH_SLAB, FLASH_TK, FLASH_SLAB), F32),
            pltpu.VMEM((2, 1, FLASH_M), F32),
            pltpu.VMEM((2, 1, FLASH_M), F32)]


def _flash_gqa(qt, k, vt):
    b, cq, s = qt.shape
    g = A_HEADS // A_KV_HEADS
    tq = FLASH_M // g
    return pl.pallas_call(
        functools.partial(_flash_kernel, gqa=True, tk=FLASH_TK, q_axis=1),
        grid=(b, s // tq),
        in_specs=[pl.BlockSpec((1, cq, tq), lambda i, j: (i, 0, j)),
                  pl.BlockSpec((1, cq, tq), lambda i, j: (i, 0, jnp.minimum(j + 1, s // tq - 1))),
                  pl.BlockSpec((1, s, A_KV), lambda i, j: (i, 0, 0)),
                  pl.BlockSpec((1, A_KV, s), lambda i, j: (i, 0, 0))],
        out_specs=pl.BlockSpec((1, cq, tq), lambda i, j: (i, 0, j)),
        out_shape=jax.ShapeDtypeStruct((b, cq, s), BF16),
        scratch_shapes=_flash_scratch(2 * HEAD_DIM),
        compiler_params=pltpu.CompilerParams(
            dimension_semantics=("arbitrary", "arbitrary"), vmem_limit_bytes=VMEM_LIMIT),
        name="flash_gqa",
    )(qt, qt, k, vt)


def _flash_mla(qt, k, vt):
    b, _, s = qt.shape
    tq = FLASH_M
    pairs = B_HEADS // 2
    return pl.pallas_call(
        functools.partial(_flash_kernel, gqa=False, tk=FLASH_TK, q_axis=2),
        grid=(b, pairs, s // tq),
        in_specs=[pl.BlockSpec((1, 2 * B_QK_PAD, tq), lambda i, h, j: (i, h, j)),
                  pl.BlockSpec((1, 2 * B_QK_PAD, tq), lambda i, h, j: (i, h, jnp.minimum(j + 1, s // tq - 1))),
                  pl.BlockSpec((1, s, 2 * B_QK_PAD), lambda i, h, j: (i, 0, h)),
                  pl.BlockSpec((1, 2 * B_V, s), lambda i, h, j: (i, h, 0))],
        out_specs=pl.BlockSpec((1, 2 * B_V, tq), lambda i, h, j: (i, h, j)),
        out_shape=jax.ShapeDtypeStruct((b, B_HEADS * B_V, s), BF16),
        scratch_shapes=_flash_scratch(B_QK_PAD),
        compiler_params=pltpu.CompilerParams(
            dimension_semantics=("arbitrary", "arbitrary", "arbitrary"), vmem_limit_bytes=VMEM_LIMIT),
        name="flash_mla",
    )(qt, qt, k, vt)


def _mlp_kernel(*refs, n_mix, final):
    h_ref = refs[0]
    mix_refs = refs[1:1 + 2 * n_mix]
    gffn_ref, wup_ref, wdown_ref = refs[1 + 2 * n_mix:4 + 2 * n_mix]
    rest = refs[4 + 2 * n_mix:]
    gfin_ref = rest[0] if final else None
    o_ref = rest[-1]

    at = None
    for i in range(n_mix):
        part = _dot(mix_refs[2 * i][...], mix_refs[2 * i + 1][0])
        at = part if at is None else at + part
    h1 = h_ref[0] + at.T
    hn = _rms_rows(h1, gffn_ref[...]).astype(BF16)
    acc = h1
    n_chunks = D_FF // D_MODEL
    for c in range(n_chunks):
        u = jnp.maximum(_dot(hn, wup_ref[:, c * D_MODEL:(c + 1) * D_MODEL]), 0.0)
        acc = acc + _dot((u * u).astype(BF16), wdown_ref[c * D_MODEL:(c + 1) * D_MODEL, :])
    if final:
        acc = _rms_rows(acc, gfin_ref[...])
    o_ref[0] = acc


def _outproj_mlp(h, mixes, gffn, wup, wdown, gfin=None):
    b, s, d = h.shape
    tm = MLP_TM
    args = [h]
    in_specs = [pl.BlockSpec((1, tm, d), lambda i, j: (i, j, 0))]
    for wt, ot in mixes:
        args += [wt, ot]
        in_specs += [_const_spec(wt.shape), pl.BlockSpec((1, ot.shape[1], tm), lambda i, j: (i, 0, j))]
    args += [gffn, wup, wdown]
    in_specs += [_const_spec(gffn.shape), _const_spec(wup.shape), _const_spec(wdown.shape)]
    if gfin is not None:
        args.append(gfin)
        in_specs.append(_const_spec(gfin.shape))
    return pl.pallas_call(
        functools.partial(_mlp_kernel, n_mix=len(mixes), final=gfin is not None),
        grid=(b, s // tm),
        in_specs=in_specs,
        out_specs=pl.BlockSpec((1, tm, d), lambda i, j: (i, j, 0)),
        out_shape=jax.ShapeDtypeStruct((b, s, d), F32),
        compiler_params=pltpu.CompilerParams(
            dimension_semantics=("arbitrary", "arbitrary"), vmem_limit_bytes=VMEM_LIMIT),
        name="outproj_mlp",
    )(*args)


def _odd_proj_kernel(h_ref, g_ref, wqt_ref, wk_ref, wvt_ref, qt_ref, k_ref, vt_ref):
    hn = _rms_rows(h_ref[0], g_ref[...]).astype(BF16)
    scale = HEAD_DIM ** -0.5 * LOG2E
    qt_ref[0] = (_nt_dot(wqt_ref[...], hn) * scale).astype(BF16)
    k_ref[0] = _dot(hn, wk_ref[...]).astype(BF16)
    vt_ref[0] = _nt_dot(wvt_ref[...], hn).astype(BF16)


def _odd_proj(h, g, wqt, wk, wvt):
    b, s, d = h.shape
    tm = PROJ_TM
    c = wqt.shape[0]
    tok = pl.BlockSpec((1, c, tm), lambda i, j: (i, 0, j))
    row = pl.BlockSpec((1, tm, c), lambda i, j: (i, j, 0))
    return pl.pallas_call(
        _odd_proj_kernel,
        grid=(b, s // tm),
        in_specs=[pl.BlockSpec((1, tm, d), lambda i, j: (i, j, 0)), _const_spec(g.shape),
                  _const_spec(wqt.shape), _const_spec(wk.shape), _const_spec(wvt.shape)],
        out_specs=[tok, row, tok],
        out_shape=[jax.ShapeDtypeStruct((b, c, s), BF16),
                   jax.ShapeDtypeStruct((b, s, c), BF16),
                   jax.ShapeDtypeStruct((b, c, s), BF16)],
        compiler_params=pltpu.CompilerParams(
            dimension_semantics=("arbitrary", "arbitrary"), vmem_limit_bytes=VMEM_LIMIT),
        name="odd_proj",
    )(h, g, wqt, wk, wvt)


def _nat_block_geometry(rows):
    nblk = rows // NAT_QROWS
    kinds = []
    for blk in (0, 1, nblk - 1):
        r0 = blk * NAT_QROWS
        ws = min(max(r0 - C_WIN_ROWS // 2, 0), rows - NAT_KROWS)
        table = []
        for kr_l in range(NAT_KROWS):
            line = []
            for qr_l in range(NAT_QROWS):
                qr, kr = r0 + qr_l, ws + kr_l
                rs = min(max(qr - C_WIN_ROWS // 2, 0), rows - C_WIN_ROWS)
                line.append(kr - qr + C_WIN_ROWS - 1 if rs <= kr < rs + C_WIN_ROWS else None)
            table.append(line)
        kinds.append(table)
    return kinds


def _nat_kernel(qt_ref, k_ref, vt_ref, tab_ref, o_ref, bias_ref, sa_ref, sb_ref, mxa_ref, mxb_ref,
                *, rows):
    w = GRID_W
    nq = NAT_QROWS * w
    nblk = rows // NAT_QROWS
    assert nblk >= 4 and nblk % 2 == 0 and rows >= NAT_KROWS
    kinds = _nat_block_geometry(rows)

    @pl.when(pl.program_id(1) == 0)
    def _():
        lane = lax.broadcasted_iota(jnp.int32, (w, 2 * w), 1)
        neg = jnp.full((w, 2 * w), NEG, F32)
        for j in range(2):
            for t, table in enumerate(kinds):
                for kr_l in range(NAT_KROWS):
                    for qp in range(NAT_QROWS // 2):
                        ia, ib = table[kr_l][2 * qp], table[kr_l][2 * qp + 1]
                        ta = neg if ia is None else tab_ref[j, ia]
                        tb = neg if ib is None else tab_ref[j, ib]
                        tile = neg if (ia is None and ib is None) else jnp.where(lane < w, ta, tb)
                        bias_ref[j, t, kr_l * w:(kr_l + 1) * w, qp * 2 * w:(qp + 1) * 2 * w] = tile

    nqp = NAT_QROWS // 2
    lw = 2 * w

    def live(kind, kr_l, qp):
        line = kinds[kind][kr_l]
        return line[2 * qp] is not None or line[2 * qp + 1] is not None

    def key_rows(kind):
        rows_live = [r for r in range(NAT_KROWS) if any(live(kind, r, qp) for qp in range(nqp))]
        lo, hi = rows_live[0], rows_live[-1] + 1
        assert rows_live == list(range(lo, hi))
        hi_pv = lo + 2 * ((hi - lo + 1) // 2)
        assert hi_pv <= NAT_KROWS
        return lo, hi, hi_pv

    def scores(offs, kind, j, s_ref, mx_ref):
        q_off, k_off = offs
        lo, hi, _ = key_rows(kind)
        kw = k_ref[0, pl.ds(k_off + lo * w, (hi - lo) * w), :]
        q = qt_ref[0, j * HEAD_DIM:(j + 1) * HEAD_DIM, pl.ds(q_off, nq)]
        zeros = jnp.zeros_like(q)
        qz = jnp.concatenate([q, zeros] if j == 0 else [zeros, q], axis=0)
        s = _dot(kw, qz)
        for qp in range(nqp):
            cols = slice(qp * lw, (qp + 1) * lw)
            mx = None
            for kr_l in range(lo, hi):
                if live(kind, kr_l, qp):
                    r = slice(kr_l * w, (kr_l + 1) * w)
                    blk = s[(kr_l - lo) * w:(kr_l - lo + 1) * w, cols] + bias_ref[j, kind, r, cols]
                    s_ref[j, r, cols] = blk
                    mx = blk if mx is None else jnp.maximum(mx, blk)
            mx_ref[j, :, cols] = jnp.max(mx, axis=0, keepdims=True)

    def softmax_pv(offs, kind, j, s_ref, mx_ref):
        q_off, k_off = offs
        lo, _, hi_pv = key_rows(kind)
        p_cols = []
        for qp in range(nqp):
            cols = slice(qp * lw, (qp + 1) * lw)
            m = mx_ref[j, :, cols]
            pieces = []
            for kr_l in range(lo, hi_pv):
                if live(kind, kr_l, qp):
                    pieces.append(jnp.exp2(s_ref[j, kr_l * w:(kr_l + 1) * w, cols] - m).astype(BF16))
                else:
                    pieces.append(jnp.zeros((w, lw), BF16))
            p_cols.append(jnp.concatenate(pieces, axis=0))
        p = jnp.concatenate(p_cols, axis=1)
        nkp = (hi_pv - lo) * w
        vw = jnp.concatenate(
            [vt_ref[0, j * HEAD_DIM:(j + 1) * HEAD_DIM, pl.ds(k_off + lo * w, nkp)],
             jnp.ones((NAT_ACC_ROWS - HEAD_DIM, nkp), BF16)], axis=0)
        pv = _dot(vw, p)
        o = pv[:HEAD_DIM] / pv[HEAD_DIM:HEAD_DIM + 1]
        o_ref[0, j * HEAD_DIM:(j + 1) * HEAD_DIM, pl.ds(q_off, nq)] = o.astype(BF16)

    half = (C_WIN_ROWS // 2) * w

    def desc(t):
        if isinstance(t, int):
            if t == 0:
                return (0, 0), 0
            if t == nblk - 1:
                return ((nblk - 1) * nq, (rows - NAT_KROWS) * w), 2
            return (t * nq, t * nq - half), 1
        return (pl.multiple_of(t * nq, nq), pl.multiple_of(t * nq - half, half)), 1

    bufs = ((sa_ref, mxa_ref), (sb_ref, mxb_ref))

    def step(t_next, par_next, t_cur):
        for j in range(2):
            if t_next is not None:
                offs, kind = desc(t_next)
                scores(offs, kind, j, *bufs[par_next])
            if t_cur is not None:
                offs, kind = desc(t_cur)
                softmax_pv(offs, kind, j, *bufs[1 - par_next])

    step(0, 0, None)
    step(1, 1, 0)
    iters = (nblk - 3) // NAT_UNROLL

    def body(i, carry):
        t0 = 2 + NAT_UNROLL * i
        for u in range(NAT_UNROLL):
            step(t0 + u, u % 2, t0 + u - 1)
        return carry

    if iters:
        lax.fori_loop(0, iters, body, 0)
    for t in range(2 + NAT_UNROLL * iters, nblk):
        step(t, t % 2, t - 1)
    step(None, nblk % 2, nblk - 1)


def _natten(qt, k, vt, tab):
    b, c, s = qt.shape
    rows = s // GRID_W
    pairs = C_HEADS // 2
    pc = 2 * HEAD_DIM
    nrel = 2 * C_WIN_ROWS - 1
    return pl.pallas_call(
        functools.partial(_nat_kernel, rows=rows),
        grid=(pairs, b),
        in_specs=[pl.BlockSpec((1, pc, s), lambda h, i: (i, h, 0)),
                  pl.BlockSpec((1, s, pc), lambda h, i: (i, 0, h)),
                  pl.BlockSpec((1, pc, s), lambda h, i: (i, h, 0)),
                  pl.BlockSpec((2, nrel, GRID_W, 2 * GRID_W), lambda h, i: (h, 0, 0, 0))],
        out_specs=pl.BlockSpec((1, pc, s), lambda h, i: (i, h, 0)),
        out_shape=jax.ShapeDtypeStruct((b, c, s), BF16),
        scratch_shapes=[pltpu.VMEM((2, 3, NAT_KROWS * GRID_W, NAT_QROWS * GRID_W), F32),
                        pltpu.VMEM((2, NAT_KROWS * GRID_W, NAT_QROWS * GRID_W), F32),
                        pltpu.VMEM((2, NAT_KROWS * GRID_W, NAT_QROWS * GRID_W), F32),
                        pltpu.VMEM((2, 1, NAT_QROWS * GRID_W), F32),
                        pltpu.VMEM((2, 1, NAT_QROWS * GRID_W), F32)],
        compiler_params=pltpu.CompilerParams(
            dimension_semantics=("arbitrary", "arbitrary"), vmem_limit_bytes=VMEM_LIMIT),
        name="natten",
    )(qt, k, vt, tab)


def _nat_bias_tiles(rpb):
    kc = np.arange(GRID_W)[:, None]
    qc = np.arange(GRID_W)[None, :]
    cs = np.clip(qc - C_WIN_COLS // 2, 0, GRID_W - C_WIN_COLS)
    valid = (kc >= cs) & (kc < cs + C_WIN_COLS)
    rel = kc - qc + (C_WIN_COLS - 1)
    onehot = (rel[None] == np.arange(2 * C_WIN_COLS - 1)[:, None, None]).astype(np.float32)
    tiles = jnp.einsum("had,dkq->hakq", rpb * LOG2E, onehot, precision=lax.Precision.HIGHEST)
    tiles = jnp.where(valid[None, None], tiles, NEG)
    return jnp.concatenate([tiles, tiles], axis=-1).astype(F32)


def _even_layer_weights(w_in, w_uq, w_ukv):
    qa, ka, va, cq, ckv, kr = jnp.split(
        w_in, (A_Q, A_Q + A_KV, A_Q + 2 * A_KV, A_Q + 2 * A_KV + B_Q_RANK,
               A_Q + 2 * A_KV + B_Q_RANK + B_KV_RANK), axis=1)
    w1t = jnp.concatenate([qa, ka, va, cq, kr], axis=1).T.astype(BF16)
    w2 = ckv.astype(BF16)
    uq = w_uq.reshape(B_Q_RANK, B_HEADS, B_NOPE + B_ROPE)
    uq = jnp.pad(uq, ((0, 0), (0, 0), (0, B_QK_PAD - B_NOPE - B_ROPE)))
    wuqt = uq.reshape(B_Q_RANK, B_HEADS * B_QK_PAD).T.astype(BF16)
    ukv = w_ukv.reshape(B_KV_RANK, B_HEADS, B_NOPE + B_V)
    wk = jnp.pad(ukv[:, :, :B_NOPE], ((0, 0), (0, 0), (0, B_QK_PAD - B_NOPE)))
    wk = wk.reshape(B_KV_RANK, B_HEADS * B_QK_PAD).astype(BF16)
    wvt = ukv[:, :, B_NOPE:].reshape(B_KV_RANK, B_HEADS * B_V).T.astype(BF16)
    return w1t, w2, wuqt, wk, wvt


def kernel(x, norm_mix, ev_w_in, ev_a_q_norm, ev_a_k_norm, ev_b_q_norm, ev_b_w_uq, ev_b_kv_norm,
           ev_b_w_ukv, ev_w_out, od_w_qkv, od_rpb, od_w_out, norm_ffn, ffn_w_up, ffn_w_down,
           final_norm):
    depth = norm_mix.shape[0]
    s = x.shape[1]
    ropea = _rope_tables(s, HEAD_DIM)
    ropeb = _rope_tables(s, B_ROPE)
    h = x
    for layer in range(depth):
        i = layer // 2
        gmix = norm_mix[layer][None, :]
        if layer % 2 == 0:
            w1t, w2, wuqt, wk, wvt = _even_layer_weights(ev_w_in[i], ev_b_w_uq[i], ev_b_w_ukv[i])
            qat, ka, vat, qbt, kb, vbt = _even_proj(
                h, gmix, w1t, w2, ev_a_q_norm[i][:, None], ev_a_k_norm[i][:, None],
                ev_b_q_norm[i][:, None], ev_b_kv_norm[i][None, :], wuqt, wk, wvt, ropea, ropeb)
            oat = _flash_gqa(qat, ka, vat)
            obt = _flash_mla(qbt, kb, vbt)
            wot = ev_w_out[i].T.astype(BF16)
            mixes = [(wot[:, :A_Q], oat), (wot[:, A_Q:], obt)]
        else:
            wq, wk_, wv = jnp.split(od_w_qkv[i], 3, axis=1)
            qt, k, vt = _odd_proj(h, gmix, wq.T.astype(BF16), wk_.astype(BF16), wv.T.astype(BF16))
            ot = _natten(qt, k, vt, _nat_bias_tiles(od_rpb[i]))
            mixes = [(od_w_out[i].T.astype(BF16), ot)]
        last = layer == depth - 1
        h = _outproj_mlp(h, mixes, norm_ffn[layer][None, :], ffn_w_up[layer].astype(BF16),
                         ffn_w_down[layer].astype(BF16), final_norm[None, :] if last else None)
    return h
```

```python
import functools
import math

import numpy as np

import jax
import jax.numpy as jnp
from jax import lax
from jax.experimental import pallas as pl
from jax.experimental.pallas import tpu as pltpu

D_MODEL = 1024
GRID_W = 64
HEAD_DIM = 64
ROPE_THETA = 10000.0
EPS = 1e-6
A_HEADS = 8
A_KV_HEADS = 2
B_HEADS = 8
B_Q_RANK = 384
B_KV_RANK = 256
B_NOPE = 64
B_ROPE = 32
B_V = 64
C_HEADS = 16
C_WIN_ROWS = 8
C_WIN_COLS = 16
D_FF = 4 * D_MODEL

A_Q = A_HEADS * HEAD_DIM
A_KV = A_KV_HEADS * HEAD_DIM
B_QK_PAD = 128
LOG2E = math.log2(math.e)
NEG = -1e30

BF16 = jnp.bfloat16
F32 = jnp.float32

VMEM_LIMIT = 56 * 1024 * 1024

PROJ_TM = 1024
MLP_TM = 512
FLASH_TK = 512
FLASH_M = 1024
FLASH_SLAB = 256
FLASH_UNROLL = 6
FLASH_ACC_ROWS = HEAD_DIM + 16
NAT_QROWS = 4
NAT_KROWS = 12
NAT_UNROLL = 8
NAT_ACC_ROWS = HEAD_DIM + 16


def _const_spec(shape):
    nd = len(shape)
    return pl.BlockSpec(shape, lambda *_: (0,) * nd, pipeline_mode=pl.Buffered(1))


def _nt_dot(a, b):
    return lax.dot_general(a, b, (((1,), (1,)), ((), ())), preferred_element_type=F32)


def _dot(a, b):
    return jnp.dot(a, b, preferred_element_type=F32)


def _rms_rows(x, g):
    ms = jnp.mean(x * x, axis=-1, keepdims=True)
    return x * lax.rsqrt(ms + EPS) * g


def _rms_cols(xT, g):
    ms = jnp.mean(xT * xT, axis=0, keepdims=True)
    return xT * lax.rsqrt(ms + EPS) * g


def _rope_cols(xT, cos, sin, bs):
    d = xT.shape[0]
    parts = [xT[i * bs:(i + 1) * bs] for i in range(d // bs)]
    swapped = jnp.concatenate([parts[i ^ 1] for i in range(len(parts))], axis=0)
    return xT * cos + swapped * sin


def _rope_tables(seq, d):
    half = d // 2
    t = jnp.arange(seq, dtype=jnp.int32)
    row = (t // GRID_W).astype(F32)
    col = (t % GRID_W).astype(F32)
    inv = ROPE_THETA ** (-jnp.arange(0, half, 2, dtype=F32) / half)
    ang_r = (row[:, None] * inv[None, :]).T
    ang_c = (col[:, None] * inv[None, :]).T
    cos = jnp.concatenate([jnp.cos(ang_r), jnp.cos(ang_r), jnp.cos(ang_c), jnp.cos(ang_c)], axis=0)
    sin = jnp.concatenate([-jnp.sin(ang_r), jnp.sin(ang_r), -jnp.sin(ang_c), jnp.sin(ang_c)], axis=0)
    return jnp.stack([cos, sin])


def _even_proj_kernel(x_ref, gmix_ref, w1t_ref, w2_ref, gaq_ref, gak_ref, gbq_ref, gbkv_ref,
                      wuqt_ref, wk_ref, wvt_ref, ropea_ref, ropeb_ref,
                      qat_ref, ka_ref, vat_ref, qbt_ref, kb_ref, vbt_ref):
    x = x_ref[0]
    hn = _rms_rows(x, gmix_ref[...]).astype(BF16)
    zt = _nt_dot(w1t_ref[...], hn)
    z2 = _dot(hn, w2_ref[...])

    cos_a, sin_a = ropea_ref[0], ropea_ref[1]
    cos_b, sin_b = ropeb_ref[0], ropeb_ref[1]
    scale_a = HEAD_DIM ** -0.5 * LOG2E
    scale_b = (B_NOPE + B_ROPE) ** -0.5 * LOG2E

    gaq = gaq_ref[...]
    for h in range(A_HEADS):
        blk = zt[h * HEAD_DIM:(h + 1) * HEAD_DIM]
        q = _rope_cols(_rms_cols(blk, gaq), cos_a, sin_a, HEAD_DIM // 4) * scale_a
        qat_ref[0, h * HEAD_DIM:(h + 1) * HEAD_DIM, :] = q.astype(BF16)

    gak = gak_ref[...]
    kts = []
    for h in range(A_KV_HEADS):
        blk = zt[A_Q + h * HEAD_DIM:A_Q + (h + 1) * HEAD_DIM]
        kts.append(_rope_cols(_rms_cols(blk, gak), cos_a, sin_a, HEAD_DIM // 4))
    ka_ref[0] = jnp.concatenate(kts, axis=0).T.astype(BF16)

    vat_ref[0] = zt[A_Q + A_KV:A_Q + 2 * A_KV].astype(BF16)

    c0 = A_Q + 2 * A_KV
    cqn = _rms_cols(zt[c0:c0 + B_Q_RANK], gbq_ref[...]).astype(BF16)
    qbt = _dot(wuqt_ref[...], cqn)
    for h in range(B_HEADS):
        blk = qbt[h * B_QK_PAD:(h + 1) * B_QK_PAD]
        roped = _rope_cols(blk[B_NOPE:B_NOPE + B_ROPE], cos_b, sin_b, B_ROPE // 4)
        q = jnp.concatenate([blk[:B_NOPE], roped, blk[B_NOPE + B_ROPE:]], axis=0) * scale_b
        qbt_ref[0, h * B_QK_PAD:(h + 1) * B_QK_PAD, :] = q.astype(BF16)

    k0 = c0 + B_Q_RANK
    kr = _rope_cols(zt[k0:k0 + B_ROPE], cos_b, sin_b, B_ROPE // 4)
    tm = kr.shape[1]
    krp = jnp.concatenate([jnp.zeros((B_NOPE, tm), F32), kr,
                           jnp.zeros((B_QK_PAD - B_NOPE - B_ROPE, tm), F32)], axis=0).T

    kvn = _rms_rows(z2, gbkv_ref[...]).astype(BF16)
    kn = _dot(kvn, wk_ref[...])
    for h in range(B_HEADS):
        kb_ref[0, :, h * B_QK_PAD:(h + 1) * B_QK_PAD] = (
            kn[:, h * B_QK_PAD:(h + 1) * B_QK_PAD] + krp).astype(BF16)
    vbt_ref[0] = _nt_dot(wvt_ref[...], kvn).astype(BF16)


def _even_proj(x, gmix, w1t, w2, gaq, gak, gbq, gbkv, wuqt, wk, wvt, ropea, ropeb):
    b, s, d = x.shape
    tm = PROJ_TM
    tok = lambda c: pl.BlockSpec((1, c, tm), lambda i, j: (i, 0, j))
    row = lambda c: pl.BlockSpec((1, tm, c), lambda i, j: (i, j, 0))
    return pl.pallas_call(
        _even_proj_kernel,
        grid=(b, s // tm),
        in_specs=[row(d), _const_spec(gmix.shape), _const_spec(w1t.shape), _const_spec(w2.shape),
                  _const_spec(gaq.shape), _const_spec(gak.shape), _const_spec(gbq.shape),
                  _const_spec(gbkv.shape), _const_spec(wuqt.shape), _const_spec(wk.shape),
                  _const_spec(wvt.shape),
                  pl.BlockSpec((2, HEAD_DIM, tm), lambda i, j: (0, 0, j)),
                  pl.BlockSpec((2, B_ROPE, tm), lambda i, j: (0, 0, j))],
        out_specs=[tok(A_Q), row(A_KV), tok(A_KV), tok(B_HEADS * B_QK_PAD),
                   row(B_HEADS * B_QK_PAD), tok(B_HEADS * B_V)],
        out_shape=[jax.ShapeDtypeStruct((b, A_Q, s), BF16),
                   jax.ShapeDtypeStruct((b, s, A_KV), BF16),
                   jax.ShapeDtypeStruct((b, A_KV, s), BF16),
                   jax.ShapeDtypeStruct((b, B_HEADS * B_QK_PAD, s), BF16),
                   jax.ShapeDtypeStruct((b, s, B_HEADS * B_QK_PAD), BF16),
                   jax.ShapeDtypeStruct((b, B_HEADS * B_V, s), BF16)],
        compiler_params=pltpu.CompilerParams(
            dimension_semantics=("arbitrary", "arbitrary"), vmem_limit_bytes=VMEM_LIMIT),
        name="even_proj",
    )(x, gmix, w1t, w2, gaq, gak, gbq, gbkv, wuqt, wk, wvt, ropea, ropeb)


def _flash_kernel(qt_ref, qtn_ref, k_ref, vt_ref, o_ref, qz_ref, m_ref, acc_ref,
                  sa_ref, sb_ref, mxa_ref, mxb_ref, *, gqa, tk, q_axis):
    seq = k_ref.shape[1]
    tq = qt_ref.shape[2]
    m_cols = qz_ref.shape[3]
    dv = HEAD_DIM
    n = seq // tk
    assert n % 2 == 0
    jq = pl.program_id(q_axis)
    slot = jq % 2

    def stage(src_ref, dst_slot):
        for st in range(2):
            if gqa:
                g = m_cols // tq
                qcat = jnp.concatenate(
                    [src_ref[0, (st * g + i) * HEAD_DIM:(st * g + i + 1) * HEAD_DIM, :] for i in range(g)],
                    axis=1)
                zeros = jnp.zeros_like(qcat)
                qz_ref[dst_slot, st] = jnp.concatenate([qcat, zeros] if st == 0 else [zeros, qcat], axis=0)
            else:
                qz_ref[dst_slot, st] = src_ref[0, st * B_QK_PAD:(st + 1) * B_QK_PAD, :]

    slabs = [(st, sl * FLASH_SLAB) for st in range(2) for sl in range(m_cols // FLASH_SLAB)]

    def k_chunk(c, st):
        off = pl.multiple_of(c * tk, tk)
        if gqa:
            return k_ref[0, pl.ds(off, tk), :]
        return k_ref[0, pl.ds(off, tk), st * B_QK_PAD:(st + 1) * B_QK_PAD]

    ones = jnp.ones((FLASH_ACC_ROWS - dv, tk), BF16)

    def v_chunk(c, st):
        off = pl.multiple_of(c * tk, tk)
        return jnp.concatenate([vt_ref[0, st * dv:(st + 1) * dv, pl.ds(off, tk)], ones], axis=0)

    def scores(kc, q_slot, st, col, s_ref, mx_ref):
        cols = slice(col, col + FLASH_SLAB)
        s = _dot(kc, qz_ref[q_slot, st, :, cols])
        s_ref[st, col // FLASH_SLAB] = s
        mx_ref[st, :, cols] = jnp.max(s, axis=0, keepdims=True)

    def softmax_pv(vc, st, col, s_ref, mx_ref):
        cols = slice(col, col + FLASH_SLAB)
        m_prev = m_ref[st, :, cols]
        m_new = jnp.maximum(m_prev, mx_ref[st, :, cols])
        alpha = jnp.exp2(m_prev - m_new)
        p = jnp.exp2(s_ref[st, col // FLASH_SLAB] - m_new).astype(BF16)
        acc_ref[st, :, cols] = alpha * acc_ref[st, :, cols] + _dot(vc, p)
        m_ref[st, :, cols] = m_new

    def step(c_next, q_slot, nxt, c_cur, cur):
        kcs = [k_chunk(c_next, st) for st in range(2)] if nxt is not None else None
        vcs = [v_chunk(c_cur, st) for st in range(2)] if cur is not None else None
        for st, col in slabs:
            if nxt is not None:
                scores(kcs[st], q_slot, st, col, *nxt)
            if cur is not None:
                softmax_pv(vcs[st], st, col, *cur)

    bufs = ((sa_ref, mxa_ref), (sb_ref, mxb_ref))

    @pl.when(jq == 0)
    def _():
        stage(qt_ref, 0)
        step(0, 0, bufs[0], None, None)

    stage(qtn_ref, 1 - slot)
    m_ref[...] = jnp.full(m_ref.shape, NEG, F32)
    acc_ref[...] = jnp.zeros(acc_ref.shape, F32)
    iters = (n - 1) // FLASH_UNROLL

    def body(i, carry):
        c0 = 1 + FLASH_UNROLL * i
        for u in range(FLASH_UNROLL):
            step(c0 + u, slot, bufs[(1 + u) % 2], c0 + u - 1, bufs[u % 2])
        return carry

    if iters:
        lax.fori_loop(0, iters, body, 0)
    for c in range(1 + FLASH_UNROLL * iters, n):
        step(c, slot, bufs[c % 2], c - 1, bufs[(c - 1) % 2])
    step(0, 1 - slot, bufs[0], n - 1, bufs[(n - 1) % 2])

    for st in range(2):
        acc = acc_ref[st]
        o = acc[:dv] / acc[dv:dv + 1]
        if gqa:
            g = m_cols // tq
            for i in range(g):
                o_ref[0, (st * g + i) * dv:(st * g + i + 1) * dv, :] = o[:, i * tq:(i + 1) * tq].astype(BF16)
        else:
            o_ref[0, st * dv:(st + 1) * dv, :] = o.astype(BF16)


def _flash_scratch(dk):
    return [pltpu.VMEM((2, 2, dk, FLASH_M), BF16),
            pltpu.VMEM((2, 1, FLASH_M), F32),
            pltpu.VMEM((2, FLASH_ACC_ROWS, FLASH_M), F32),
            pltpu.VMEM((2, FLASH_M // FLASH_SLAB, FLASH_TK, FLASH_SLAB), F32),
            pltpu.VMEM((2, FLASH_M // FLASH_SLAB, FLASH_TK, FLASH_SLAB), F32),
            pltpu.VMEM((2, 1, FLASH_M), F32),
            pltpu.VMEM((2, 1, FLASH_M), F32)]


def _flash_gqa(qt, k, vt):
    b, cq, s = qt.shape
    g = A_HEADS // A_KV_HEADS
    tq = FLASH_M // g
    return pl.pallas_call(
        functools.partial(_flash_kernel, gqa=True, tk=FLASH_TK, q_axis=1),
        grid=(b, s // tq),
        in_specs=[pl.BlockSpec((1, cq, tq), lambda i, j: (i, 0, j)),
                  pl.BlockSpec((1, cq, tq), lambda i, j: (i, 0, jnp.minimum(j + 1, s // tq - 1))),
                  pl.BlockSpec((1, s, A_KV), lambda i, j: (i, 0, 0)),
                  pl.BlockSpec((1, A_KV, s), lambda i, j: (i, 0, 0))],
        out_specs=pl.BlockSpec((1, cq, tq), lambda i, j: (i, 0, j)),
        out_shape=jax.ShapeDtypeStruct((b, cq, s), BF16),
        scratch_shapes=_flash_scratch(2 * HEAD_DIM),
        compiler_params=pltpu.CompilerParams(
            dimension_semantics=("arbitrary", "arbitrary"), vmem_limit_bytes=VMEM_LIMIT),
        name="flash_gqa",
    )(qt, qt, k, vt)


def _flash_mla(qt, k, vt):
    b, _, s = qt.shape
    tq = FLASH_M
    pairs = B_HEADS // 2
    return pl.pallas_call(
        functools.partial(_flash_kernel, gqa=False, tk=FLASH_TK, q_axis=2),
        grid=(b, pairs, s // tq),
        in_specs=[pl.BlockSpec((1, 2 * B_QK_PAD, tq), lambda i, h, j: (i, h, j)),
                  pl.BlockSpec((1, 2 * B_QK_PAD, tq), lambda i, h, j: (i, h, jnp.minimum(j + 1, s // tq - 1))),
                  pl.BlockSpec((1, s, 2 * B_QK_PAD), lambda i, h, j: (i, 0, h)),
                  pl.BlockSpec((1, 2 * B_V, s), lambda i, h, j: (i, h, 0))],
        out_specs=pl.BlockSpec((1, 2 * B_V, tq), lambda i, h, j: (i, h, j)),
        out_shape=jax.ShapeDtypeStruct((b, B_HEADS * B_V, s), BF16),
        scratch_shapes=_flash_scratch(B_QK_PAD),
        compiler_params=pltpu.CompilerParams(
            dimension_semantics=("arbitrary", "arbitrary", "arbitrary"), vmem_limit_bytes=VMEM_LIMIT),
        name="flash_mla",
    )(qt, qt, k, vt)


def _mlp_kernel(*refs, n_mix, final):
    h_ref = refs[0]
    mix_refs = refs[1:1 + 2 * n_mix]
    gffn_ref, wup_ref, wdown_ref = refs[1 + 2 * n_mix:4 + 2 * n_mix]
    rest = refs[4 + 2 * n_mix:]
    gfin_ref = rest[0] if final else None
    o_ref = rest[-1]

    at = None
    for i in range(n_mix):
        part = _dot(mix_refs[2 * i][...], mix_refs[2 * i + 1][0])
        at = part if at is None else at + part
    h1 = h_ref[0] + at.T
    hn = _rms_rows(h1, gffn_ref[...]).astype(BF16)
    acc = h1
    n_chunks = D_FF // D_MODEL
    for c in range(n_chunks):
        u = jnp.maximum(_dot(hn, wup_ref[:, c * D_MODEL:(c + 1) * D_MODEL]), 0.0)
        acc = acc + _dot((u * u).astype(BF16), wdown_ref[c * D_MODEL:(c + 1) * D_MODEL, :])
    if final:
        acc = _rms_rows(acc, gfin_ref[...])
    o_ref[0] = acc


def _outproj_mlp(h, mixes, gffn, wup, wdown, gfin=None):
    b, s, d = h.shape
    tm = MLP_TM
    args = [h]
    in_specs = [pl.BlockSpec((1, tm, d), lambda i, j: (i, j, 0))]
    for wt, ot in mixes:
        args += [wt, ot]
        in_specs += [_const_spec(wt.shape), pl.BlockSpec((1, ot.shape[1], tm), lambda i, j: (i, 0, j))]
    args += [gffn, wup, wdown]
    in_specs += [_const_spec(gffn.shape), _const_spec(wup.shape), _const_spec(wdown.shape)]
    if gfin is not None:
        args.append(gfin)
        in_specs.append(_const_spec(gfin.shape))
    return pl.pallas_call(
        functools.partial(_mlp_kernel, n_mix=len(mixes), final=gfin is not None),
        grid=(b, s // tm),
        in_specs=in_specs,
        out_specs=pl.BlockSpec((1, tm, d), lambda i, j: (i, j, 0)),
        out_shape=jax.ShapeDtypeStruct((b, s, d), F32),
        compiler_params=pltpu.CompilerParams(
            dimension_semantics=("arbitrary", "arbitrary"), vmem_limit_bytes=VMEM_LIMIT),
        name="outproj_mlp",
    )(*args)


def _odd_proj_kernel(h_ref, g_ref, wqt_ref, wk_ref, wvt_ref, qt_ref, k_ref, vt_ref):
    hn = _rms_rows(h_ref[0], g_ref[...]).astype(BF16)
    scale = HEAD_DIM ** -0.5 * LOG2E
    qt_ref[0] = (_nt_dot(wqt_ref[...], hn) * scale).astype(BF16)
    k_ref[0] = _dot(hn, wk_ref[...]).astype(BF16)
    vt_ref[0] = _nt_dot(wvt_ref[...], hn).astype(BF16)


def _odd_proj(h, g, wqt, wk, wvt):
    b, s, d = h.shape
    tm = PROJ_TM
    c = wqt.shape[0]
    tok = pl.BlockSpec((1, c, tm), lambda i, j: (i, 0, j))
    row = pl.BlockSpec((1, tm, c), lambda i, j: (i, j, 0))
    return pl.pallas_call(
        _odd_proj_kernel,
        grid=(b, s // tm),
        in_specs=[pl.BlockSpec((1, tm, d), lambda i, j: (i, j, 0)), _const_spec(g.shape),
                  _const_spec(wqt.shape), _const_spec(wk.shape), _const_spec(wvt.shape)],
        out_specs=[tok, row, tok],
        out_shape=[jax.ShapeDtypeStruct((b, c, s), BF16),
                   jax.ShapeDtypeStruct((b, s, c), BF16),
                   jax.ShapeDtypeStruct((b, c, s), BF16)],
        compiler_params=pltpu.CompilerParams(
            dimension_semantics=("arbitrary", "arbitrary"), vmem_limit_bytes=VMEM_LIMIT),
        name="odd_proj",
    )(h, g, wqt, wk, wvt)


def _nat_block_geometry(rows):
    nblk = rows // NAT_QROWS
    kinds = []
    for blk in (0, 1, nblk - 1):
        r0 = blk * NAT_QROWS
        ws = min(max(r0 - C_WIN_ROWS // 2, 0), rows - NAT_KROWS)
        table = []
        for kr_l in range(NAT_KROWS):
            line = []
            for qr_l in range(NAT_QROWS):
                qr, kr = r0 + qr_l, ws + kr_l
                rs = min(max(qr - C_WIN_ROWS // 2, 0), rows - C_WIN_ROWS)
                line.append(kr - qr + C_WIN_ROWS - 1 if rs <= kr < rs + C_WIN_ROWS else None)
            table.append(line)
        kinds.append(table)
    return kinds


def _nat_kernel(qt_ref, k_ref, vt_ref, tab_ref, o_ref, bias_ref, sa_ref, sb_ref, mxa_ref, mxb_ref,
                *, rows):
    w = GRID_W
    nq = NAT_QROWS * w
    nblk = rows // NAT_QROWS
    assert nblk >= 4 and nblk % 2 == 0 and rows >= NAT_KROWS
    kinds = _nat_block_geometry(rows)

    @pl.when(pl.program_id(1) == 0)
    def _():
        lane = lax.broadcasted_iota(jnp.int32, (w, 2 * w), 1)
        neg = jnp.full((w, 2 * w), NEG, F32)
        for j in range(2):
            for t, table in enumerate(kinds):
                for kr_l in range(NAT_KROWS):
                    for qp in range(NAT_QROWS // 2):
                        ia, ib = table[kr_l][2 * qp], table[kr_l][2 * qp + 1]
                        ta = neg if ia is None else tab_ref[j, ia]
                        tb = neg if ib is None else tab_ref[j, ib]
                        tile = neg if (ia is None and ib is None) else jnp.where(lane < w, ta, tb)
                        bias_ref[j, t, kr_l * w:(kr_l + 1) * w, qp * 2 * w:(qp + 1) * 2 * w] = tile

    nqp = NAT_QROWS // 2
    lw = 2 * w

    def live(kind, kr_l, qp):
        line = kinds[kind][kr_l]
        return line[2 * qp] is not None or line[2 * qp + 1] is not None

    def key_rows(kind):
        rows_live = [r for r in range(NAT_KROWS) if any(live(kind, r, qp) for qp in range(nqp))]
        lo, hi = rows_live[0], rows_live[-1] + 1
        assert rows_live == list(range(lo, hi))
        hi_pv = lo + 2 * ((hi - lo + 1) // 2)
        assert hi_pv <= NAT_KROWS
        return lo, hi, hi_pv

    def scores(offs, kind, j, s_ref, mx_ref):
        q_off, k_off = offs
        lo, hi, _ = key_rows(kind)
        kw = k_ref[0, pl.ds(k_off + lo * w, (hi - lo) * w), :]
        q = qt_ref[0, j * HEAD_DIM:(j + 1) * HEAD_DIM, pl.ds(q_off, nq)]
        zeros = jnp.zeros_like(q)
        qz = jnp.concatenate([q, zeros] if j == 0 else [zeros, q], axis=0)
        s = _dot(kw, qz)
        for qp in range(nqp):
            cols = slice(qp * lw, (qp + 1) * lw)
            mx = None
            for kr_l in range(lo, hi):
                if live(kind, kr_l, qp):
                    r = slice(kr_l * w, (kr_l + 1) * w)
                    blk = s[(kr_l - lo) * w:(kr_l - lo + 1) * w, cols] + bias_ref[j, kind, r, cols]
                    s_ref[j, r, cols] = blk
                    mx = blk if mx is None else jnp.maximum(mx, blk)
            mx_ref[j, :, cols] = jnp.max(mx, axis=0, keepdims=True)

    def softmax_pv(offs, kind, j, s_ref, mx_ref):
        q_off, k_off = offs
        lo, _, hi_pv = key_rows(kind)
        p_cols = []
        for qp in range(nqp):
            cols = slice(qp * lw, (qp + 1) * lw)
            m = mx_ref[j, :, cols]
            pieces = []
            for kr_l in range(lo, hi_pv):
                if live(kind, kr_l, qp):
                    pieces.append(jnp.exp2(s_ref[j, kr_l * w:(kr_l + 1) * w, cols] - m).astype(BF16))
                else:
                    pieces.append(jnp.zeros((w, lw), BF16))
            p_cols.append(jnp.concatenate(pieces, axis=0))
        p = jnp.concatenate(p_cols, axis=1)
        nkp = (hi_pv - lo) * w
        vw = jnp.concatenate(
            [vt_ref[0, j * HEAD_DIM:(j + 1) * HEAD_DIM, pl.ds(k_off + lo * w, nkp)],
             jnp.ones((NAT_ACC_ROWS - HEAD_DIM, nkp), BF16)], axis=0)
        pv = _dot(vw, p)
        o = pv[:HEAD_DIM] / pv[HEAD_DIM:HEAD_DIM + 1]
        o_ref[0, j * HEAD_DIM:(j + 1) * HEAD_DIM, pl.ds(q_off, nq)] = o.astype(BF16)

    half = (C_WIN_ROWS // 2) * w

    def desc(t):
        if isinstance(t, int):
            if t == 0:
                return (0, 0), 0
            if t == nblk - 1:
                return ((nblk - 1) * nq, (rows - NAT_KROWS) * w), 2
            return (t * nq, t * nq - half), 1
        return (pl.multiple_of(t * nq, nq), pl.multiple_of(t * nq - half, half)), 1

    bufs = ((sa_ref, mxa_ref), (sb_ref, mxb_ref))

    def step(t_next, par_next, t_cur):
        for j in range(2):
            if t_next is not None:
                offs, kind = desc(t_next)
                scores(offs, kind, j, *bufs[par_next])
            if t_cur is not None:
                offs, kind = desc(t_cur)
                softmax_pv(offs, kind, j, *bufs[1 - par_next])

    step(0, 0, None)
    step(1, 1, 0)
    iters = (nblk - 3) // NAT_UNROLL

    def body(i, carry):
        t0 = 2 + NAT_UNROLL * i
        for u in range(NAT_UNROLL):
            step(t0 + u, u % 2, t0 + u - 1)
        return carry

    if iters:
        lax.fori_loop(0, iters, body, 0)
    for t in range(2 + NAT_UNROLL * iters, nblk):
        step(t, t % 2, t - 1)
    step(None, nblk % 2, nblk - 1)


def _natten(qt, k, vt, tab):
    b, c, s = qt.shape
    rows = s // GRID_W
    pairs = C_HEADS // 2
    pc = 2 * HEAD_DIM
    nrel = 2 * C_WIN_ROWS - 1
    return pl.pallas_call(
        functools.partial(_nat_kernel, rows=rows),
        grid=(pairs, b),
        in_specs=[pl.BlockSpec((1, pc, s), lambda h, i: (i, h, 0)),
                  pl.BlockSpec((1, s, pc), lambda h, i: (i, 0, h)),
                  pl.BlockSpec((1, pc, s), lambda h, i: (i, h, 0)),
                  pl.BlockSpec((2, nrel, GRID_W, 2 * GRID_W), lambda h, i: (h, 0, 0, 0))],
        out_specs=pl.BlockSpec((1, pc, s), lambda h, i: (i, h, 0)),
        out_shape=jax.ShapeDtypeStruct((b, c, s), BF16),
        scratch_shapes=[pltpu.VMEM((2, 3, NAT_KROWS * GRID_W, NAT_QROWS * GRID_W), F32),
                        pltpu.VMEM((2, NAT_KROWS * GRID_W, NAT_QROWS * GRID_W), F32),
                        pltpu.VMEM((2, NAT_KROWS * GRID_W, NAT_QROWS * GRID_W), F32),
                        pltpu.VMEM((2, 1, NAT_QROWS * GRID_W), F32),
                        pltpu.VMEM((2, 1, NAT_QROWS * GRID_W), F32)],
        compiler_params=pltpu.CompilerParams(
            dimension_semantics=("arbitrary", "arbitrary"), vmem_limit_bytes=VMEM_LIMIT),
        name="natten",
    )(qt, k, vt, tab)


def _nat_bias_tiles(rpb):
    kc = np.arange(GRID_W)[:, None]
    qc = np.arange(GRID_W)[None, :]
    cs = np.clip(qc - C_WIN_COLS // 2, 0, GRID_W - C_WIN_COLS)
    valid = (kc >= cs) & (kc < cs + C_WIN_COLS)
    rel = kc - qc + (C_WIN_COLS - 1)
    onehot = (rel[None] == np.arange(2 * C_WIN_COLS - 1)[:, None, None]).astype(np.float32)
    tiles = jnp.einsum("had,dkq->hakq", rpb * LOG2E, onehot, precision=lax.Precision.HIGHEST)
    tiles = jnp.where(valid[None, None], tiles, NEG)
    return jnp.concatenate([tiles, tiles], axis=-1).astype(F32)


def _even_layer_weights(w_in, w_uq, w_ukv):
    qa, ka, va, cq, ckv, kr = jnp.split(
        w_in, (A_Q, A_Q + A_KV, A_Q + 2 * A_KV, A_Q + 2 * A_KV + B_Q_RANK,
               A_Q + 2 * A_KV + B_Q_RANK + B_KV_RANK), axis=1)
    w1t = jnp.concatenate([qa, ka, va, cq, kr], axis=1).T.astype(BF16)
    w2 = ckv.astype(BF16)
    uq = w_uq.reshape(B_Q_RANK, B_HEADS, B_NOPE + B_ROPE)
    uq = jnp.pad(uq, ((0, 0), (0, 0), (0, B_QK_PAD - B_NOPE - B_ROPE)))
    wuqt = uq.reshape(B_Q_RANK, B_HEADS * B_QK_PAD).T.astype(BF16)
    ukv = w_ukv.reshape(B_KV_RANK, B_HEADS, B_NOPE + B_V)
    wk = jnp.pad(ukv[:, :, :B_NOPE], ((0, 0), (0, 0), (0, B_QK_PAD - B_NOPE)))
    wk = wk.reshape(B_KV_RANK, B_HEADS * B_QK_PAD).astype(BF16)
    wvt = ukv[:, :, B_NOPE:].reshape(B_KV_RANK, B_HEADS * B_V).T.astype(BF16)
    return w1t, w2, wuqt, wk, wvt


def kernel(x, norm_mix, ev_w_in, ev_a_q_norm, ev_a_k_norm, ev_b_q_norm, ev_b_w_uq, ev_b_kv_norm,
           ev_b_w_ukv, ev_w_out, od_w_qkv, od_rpb, od_w_out, norm_ffn, ffn_w_up, ffn_w_down,
           final_norm):
    depth = norm_mix.shape[0]
    s = x.shape[1]
    ropea = _rope_tables(s, HEAD_DIM)
    ropeb = _rope_tables(s, B_ROPE)
    h = x
    for layer in range(depth):
        i = layer // 2
        gmix = norm_mix[layer][None, :]
        if layer % 2 == 0:
            w1t, w2, wuqt, wk, wvt = _even_layer_weights(ev_w_in[i], ev_b_w_uq[i], ev_b_w_ukv[i])
            qat, ka, vat, qbt, kb, vbt = _even_proj(
                h, gmix, w1t, w2, ev_a_q_norm[i][:, None], ev_a_k_norm[i][:, None],
                ev_b_q_norm[i][:, None], ev_b_kv_norm[i][None, :], wuqt, wk, wvt, ropea, ropeb)
            oat = _flash_gqa(qat, ka, vat)
            obt = _flash_mla(qbt, kb, vbt)
            wot = ev_w_out[i].T.astype(BF16)
            mixes = [(wot[:, :A_Q], oat), (wot[:, A_Q:], obt)]
        else:
            wq, wk_, wv = jnp.split(od_w_qkv[i], 3, axis=1)
            qt, k, vt = _odd_proj(h, gmix, wq.T.astype(BF16), wk_.astype(BF16), wv.T.astype(BF16))
            ot = _natten(qt, k, vt, _nat_bias_tiles(od_rpb[i]))
            mixes = [(od_w_out[i].T.astype(BF16), ot)]
        last = layer == depth - 1
        h = _outproj_mlp(h, mixes, norm_ffn[layer][None, :], ffn_w_up[layer].astype(BF16),
                         ffn_w_down[layer].astype(BF16), final_norm[None, :] if last else None)
    return h
```

```python
import functools
import math

import numpy as np

import jax
import jax.numpy as jnp
from jax import lax
from jax.experimental import pallas as pl
from jax.experimental.pallas import tpu as pltpu

D_MODEL = 1024
GRID_W = 64
HEAD_DIM = 64
ROPE_THETA = 10000.0
EPS = 1e-6
A_HEADS = 8
A_KV_HEADS = 2
B_HEADS = 8
B_Q_RANK = 384
B_KV_RANK = 256
B_NOPE = 64
B_ROPE = 32
B_V = 64
C_HEADS = 16
C_WIN_ROWS = 8
C_WIN_COLS = 16
D_FF = 4 * D_MODEL

A_Q = A_HEADS * HEAD_DIM
A_KV = A_KV_HEADS * HEAD_DIM
B_QK_PAD = 128
LOG2E = math.log2(math.e)
NEG = -1e30

BF16 = jnp.bfloat16
F32 = jnp.float32

VMEM_LIMIT = 56 * 1024 * 1024

PROJ_TM = 1024
MLP_TM = 1024
FLASH_TK = 512
FLASH_M = 1024
FLASH_SLAB = 256
FLASH_UNROLL = 6
FLASH_ACC_ROWS = HEAD_DIM + 16
NAT_QROWS = 4
NAT_KROWS = 12
NAT_UNROLL = 8
NAT_ACC_ROWS = HEAD_DIM + 16


def _const_spec(shape):
    nd = len(shape)
    return pl.BlockSpec(shape, lambda *_: (0,) * nd, pipeline_mode=pl.Buffered(1))


def _nt_dot(a, b):
    return lax.dot_general(a, b, (((1,), (1,)), ((), ())), preferred_element_type=F32)


def _dot(a, b):
    return jnp.dot(a, b, preferred_element_type=F32)


def _rms_rows(x, g):
    ms = jnp.mean(x * x, axis=-1, keepdims=True)
    return x * lax.rsqrt(ms + EPS) * g


def _rms_cols(xT, g):
    ms = jnp.mean(xT * xT, axis=0, keepdims=True)
    return xT * lax.rsqrt(ms + EPS) * g


def _rope_cols(xT, cos, sin, bs):
    d = xT.shape[0]
    parts = [xT[i * bs:(i + 1) * bs] for i in range(d // bs)]
    swapped = jnp.concatenate([parts[i ^ 1] for i in range(len(parts))], axis=0)
    return xT * cos + swapped * sin


def _rope_tables(seq, d):
    half = d // 2
    t = jnp.arange(seq, dtype=jnp.int32)
    row = (t // GRID_W).astype(F32)
    col = (t % GRID_W).astype(F32)
    inv = ROPE_THETA ** (-jnp.arange(0, half, 2, dtype=F32) / half)
    ang_r = (row[:, None] * inv[None, :]).T
    ang_c = (col[:, None] * inv[None, :]).T
    cos = jnp.concatenate([jnp.cos(ang_r), jnp.cos(ang_r), jnp.cos(ang_c), jnp.cos(ang_c)], axis=0)
    sin = jnp.concatenate([-jnp.sin(ang_r), jnp.sin(ang_r), -jnp.sin(ang_c), jnp.sin(ang_c)], axis=0)
    return jnp.stack([cos, sin])


def _even_proj_kernel(x_ref, gmix_ref, w1t_ref, w2_ref, gaq_ref, gak_ref, gbq_ref, gbkv_ref,
                      wuqt_ref, wk_ref, wvt_ref, ropea_ref, ropeb_ref,
                      qat_ref, ka_ref, vat_ref, qbt_ref, kb_ref, vbt_ref):
    x = x_ref[0]
    hn = _rms_rows(x, gmix_ref[...]).astype(BF16)
    zt = _nt_dot(w1t_ref[...], hn)
    z2 = _dot(hn, w2_ref[...])

    cos_a, sin_a = ropea_ref[0], ropea_ref[1]
    cos_b, sin_b = ropeb_ref[0], ropeb_ref[1]
    scale_a = HEAD_DIM ** -0.5 * LOG2E
    scale_b = (B_NOPE + B_ROPE) ** -0.5 * LOG2E

    gaq = gaq_ref[...]
    for h in range(A_HEADS):
        blk = zt[h * HEAD_DIM:(h + 1) * HEAD_DIM]
        q = _rope_cols(_rms_cols(blk, gaq), cos_a, sin_a, HEAD_DIM // 4) * scale_a
        qat_ref[0, h * HEAD_DIM:(h + 1) * HEAD_DIM, :] = q.astype(BF16)

    gak = gak_ref[...]
    kts = []
    for h in range(A_KV_HEADS):
        blk = zt[A_Q + h * HEAD_DIM:A_Q + (h + 1) * HEAD_DIM]
        kts.append(_rope_cols(_rms_cols(blk, gak), cos_a, sin_a, HEAD_DIM // 4))
    ka_ref[0] = jnp.concatenate(kts, axis=0).T.astype(BF16)

    vat_ref[0] = zt[A_Q + A_KV:A_Q + 2 * A_KV].astype(BF16)

    c0 = A_Q + 2 * A_KV
    cqn = _rms_cols(zt[c0:c0 + B_Q_RANK], gbq_ref[...]).astype(BF16)
    qbt = _dot(wuqt_ref[...], cqn)
    for h in range(B_HEADS):
        blk = qbt[h * B_QK_PAD:(h + 1) * B_QK_PAD]
        roped = _rope_cols(blk[B_NOPE:B_NOPE + B_ROPE], cos_b, sin_b, B_ROPE // 4)
        q = jnp.concatenate([blk[:B_NOPE], roped, blk[B_NOPE + B_ROPE:]], axis=0) * scale_b
        qbt_ref[0, h * B_QK_PAD:(h + 1) * B_QK_PAD, :] = q.astype(BF16)

    k0 = c0 + B_Q_RANK
    kr = _rope_cols(zt[k0:k0 + B_ROPE], cos_b, sin_b, B_ROPE // 4)
    tm = kr.shape[1]
    krp = jnp.concatenate([jnp.zeros((B_NOPE, tm), F32), kr,
                           jnp.zeros((B_QK_PAD - B_NOPE - B_ROPE, tm), F32)], axis=0).T

    kvn = _rms_rows(z2, gbkv_ref[...]).astype(BF16)
    kn = _dot(kvn, wk_ref[...])
    for h in range(B_HEADS):
        kb_ref[0, :, h * B_QK_PAD:(h + 1) * B_QK_PAD] = (
            kn[:, h * B_QK_PAD:(h + 1) * B_QK_PAD] + krp).astype(BF16)
    vbt_ref[0] = _nt_dot(wvt_ref[...], kvn).astype(BF16)


def _even_proj(x, gmix, w1t, w2, gaq, gak, gbq, gbkv, wuqt, wk, wvt, ropea, ropeb):
    b, s, d = x.shape
    tm = PROJ_TM
    tok = lambda c: pl.BlockSpec((1, c, tm), lambda i, j: (i, 0, j))
    row = lambda c: pl.BlockSpec((1, tm, c), lambda i, j: (i, j, 0))
    return pl.pallas_call(
        _even_proj_kernel,
        grid=(b, s // tm),
        in_specs=[row(d), _const_spec(gmix.shape), _const_spec(w1t.shape), _const_spec(w2.shape),
                  _const_spec(gaq.shape), _const_spec(gak.shape), _const_spec(gbq.shape),
                  _const_spec(gbkv.shape), _const_spec(wuqt.shape), _const_spec(wk.shape),
                  _const_spec(wvt.shape),
                  pl.BlockSpec((2, HEAD_DIM, tm), lambda i, j: (0, 0, j)),
                  pl.BlockSpec((2, B_ROPE, tm), lambda i, j: (0, 0, j))],
        out_specs=[tok(A_Q), row(A_KV), tok(A_KV), tok(B_HEADS * B_QK_PAD),
                   row(B_HEADS * B_QK_PAD), tok(B_HEADS * B_V)],
        out_shape=[jax.ShapeDtypeStruct((b, A_Q, s), BF16),
                   jax.ShapeDtypeStruct((b, s, A_KV), BF16),
                   jax.ShapeDtypeStruct((b, A_KV, s), BF16),
                   jax.ShapeDtypeStruct((b, B_HEADS * B_QK_PAD, s), BF16),
                   jax.ShapeDtypeStruct((b, s, B_HEADS * B_QK_PAD), BF16),
                   jax.ShapeDtypeStruct((b, B_HEADS * B_V, s), BF16)],
        compiler_params=pltpu.CompilerParams(
            dimension_semantics=("arbitrary", "arbitrary"), vmem_limit_bytes=VMEM_LIMIT),
        name="even_proj",
    )(x, gmix, w1t, w2, gaq, gak, gbq, gbkv, wuqt, wk, wvt, ropea, ropeb)


def _flash_kernel(qt_ref, qtn_ref, k_ref, vt_ref, o_ref, qz_ref, m_ref, acc_ref,
                  sa_ref, sb_ref, mxa_ref, mxb_ref, *, gqa, tk, q_axis):
    seq = k_ref.shape[1]
    tq = qt_ref.shape[2]
    m_cols = qz_ref.shape[3]
    dv = HEAD_DIM
    n = seq // tk
    assert n % 2 == 0
    jq = pl.program_id(q_axis)
    slot = jq % 2

    def stage(src_ref, dst_slot):
        for st in range(2):
            if gqa:
                g = m_cols // tq
                qcat = jnp.concatenate(
                    [src_ref[0, (st * g + i) * HEAD_DIM:(st * g + i + 1) * HEAD_DIM, :] for i in range(g)],
                    axis=1)
                zeros = jnp.zeros_like(qcat)
                qz_ref[dst_slot, st] = jnp.concatenate([qcat, zeros] if st == 0 else [zeros, qcat], axis=0)
            else:
                qz_ref[dst_slot, st] = src_ref[0, st * B_QK_PAD:(st + 1) * B_QK_PAD, :]

    slabs = [(st, sl * FLASH_SLAB) for st in range(2) for sl in range(m_cols // FLASH_SLAB)]

    def k_chunk(c, st):
        off = pl.multiple_of(c * tk, tk)
        if gqa:
            return k_ref[0, pl.ds(off, tk), :]
        return k_ref[0, pl.ds(off, tk), st * B_QK_PAD:(st + 1) * B_QK_PAD]

    ones = jnp.ones((FLASH_ACC_ROWS - dv, tk), BF16)

    def v_chunk(c, st):
        off = pl.multiple_of(c * tk, tk)
        return jnp.concatenate([vt_ref[0, st * dv:(st + 1) * dv, pl.ds(off, tk)], ones], axis=0)

    def scores(kc, q_slot, st, col, s_ref, mx_ref):
        cols = slice(col, col + FLASH_SLAB)
        s = _dot(kc, qz_ref[q_slot, st, :, cols])
        s_ref[st, col // FLASH_SLAB] = s
        mx_ref[st, :, cols] = jnp.max(s, axis=0, keepdims=True)

    def softmax_pv(vc, st, col, s_ref, mx_ref):
        cols = slice(col, col + FLASH_SLAB)
        m_prev = m_ref[st, :, cols]
        m_new = jnp.maximum(m_prev, mx_ref[st, :, cols])
        alpha = jnp.exp2(m_prev - m_new)
        p = jnp.exp2(s_ref[st, col // FLASH_SLAB] - m_new).astype(BF16)
        acc_ref[st, :, cols] = alpha * acc_ref[st, :, cols] + _dot(vc, p)
        m_ref[st, :, cols] = m_new

    def step(c_next, q_slot, nxt, c_cur, cur):
        kcs = [k_chunk(c_next, st) for st in range(2)] if nxt is not None else None
        vcs = [v_chunk(c_cur, st) for st in range(2)] if cur is not None else None
        for st, col in slabs:
            if nxt is not None:
                scores(kcs[st], q_slot, st, col, *nxt)
            if cur is not None:
                softmax_pv(vcs[st], st, col, *cur)

    bufs = ((sa_ref, mxa_ref), (sb_ref, mxb_ref))

    @pl.when(jq == 0)
    def _():
        stage(qt_ref, 0)
        step(0, 0, bufs[0], None, None)

    stage(qtn_ref, 1 - slot)
    m_ref[...] = jnp.full(m_ref.shape, NEG, F32)
    acc_ref[...] = jnp.zeros(acc_ref.shape, F32)
    iters = (n - 1) // FLASH_UNROLL

    def body(i, carry):
        c0 = 1 + FLASH_UNROLL * i
        for u in range(FLASH_UNROLL):
            step(c0 + u, slot, bufs[(1 + u) % 2], c0 + u - 1, bufs[u % 2])
        return carry

    if iters:
        lax.fori_loop(0, iters, body, 0)
    for c in range(1 + FLASH_UNROLL * iters, n):
        step(c, slot, bufs[c % 2], c - 1, bufs[(c - 1) % 2])
    step(0, 1 - slot, bufs[0], n - 1, bufs[(n - 1) % 2])

    for st in range(2):
        acc = acc_ref[st]
        o = acc[:dv] / acc[dv:dv + 1]
        if gqa:
            g = m_cols // tq
            for i in range(g):
                o_ref[0, (st * g + i) * dv:(st * g + i + 1) * dv, :] = o[:, i * tq:(i + 1) * tq].astype(BF16)
        else:
            o_ref[0, st * dv:(st + 1) * dv, :] = o.astype(BF16)


def _flash_scratch(dk):
    return [pltpu.VMEM((2, 2, dk, FLASH_M), BF16),
            pltpu.VMEM((2, 1, FLASH_M), F32),
            pltpu.VMEM((2, FLASH_ACC_ROWS, FLASH_M), F32),
            pltpu.VMEM((2, FLASH_M // FLASH_SLAB, FLASH_TK, FLASH_SLAB), F32),
            pltpu.VMEM((2, FLASH_M // FLASH_SLAB, FLASH_TK, FLASH_SLAB), F32),
            pltpu.VMEM((2, 1, FLASH_M), F32),
            pltpu.VMEM((2, 1, FLASH_M), F32)]


def _flash_gqa(qt, k, vt):
    b, cq, s = qt.shape
    g = A_HEADS // A_KV_HEADS
    tq = FLASH_M // g
    return pl.pallas_call(
        functools.partial(_flash_kernel, gqa=True, tk=FLASH_TK, q_axis=1),
        grid=(b, s // tq),
        in_specs=[pl.BlockSpec((1, cq, tq), lambda i, j: (i, 0, j)),
                  pl.BlockSpec((1, cq, tq), lambda i, j: (i, 0, jnp.minimum(j + 1, s // tq - 1))),
                  pl.BlockSpec((1, s, A_KV), lambda i, j: (i, 0, 0)),
                  pl.BlockSpec((1, A_KV, s), lambda i, j: (i, 0, 0))],
        out_specs=pl.BlockSpec((1, cq, tq), lambda i, j: (i, 0, j)),
        out_shape=jax.ShapeDtypeStruct((b, cq, s), BF16),
        scratch_shapes=_flash_scratch(2 * HEAD_DIM),
        compiler_params=pltpu.CompilerParams(
            dimension_semantics=("arbitrary", "arbitrary"), vmem_limit_bytes=VMEM_LIMIT),
        name="flash_gqa",
    )(qt, qt, k, vt)


def _flash_mla(qt, k, vt):
    b, _, s = qt.shape
    tq = FLASH_M
    pairs = B_HEADS // 2
    return pl.pallas_call(
        functools.partial(_flash_kernel, gqa=False, tk=FLASH_TK, q_axis=2),
        grid=(b, pairs, s // tq),
        in_specs=[pl.BlockSpec((1, 2 * B_QK_PAD, tq), lambda i, h, j: (i, h, j)),
                  pl.BlockSpec((1, 2 * B_QK_PAD, tq), lambda i, h, j: (i, h, jnp.minimum(j + 1, s // tq - 1))),
                  pl.BlockSpec((1, s, 2 * B_QK_PAD), lambda i, h, j: (i, 0, h)),
                  pl.BlockSpec((1, 2 * B_V, s), lambda i, h, j: (i, h, 0))],
        out_specs=pl.BlockSpec((1, 2 * B_V, tq), lambda i, h, j: (i, h, j)),
        out_shape=jax.ShapeDtypeStruct((b, B_HEADS * B_V, s), BF16),
        scratch_shapes=_flash_scratch(B_QK_PAD),
        compiler_params=pltpu.CompilerParams(
            dimension_semantics=("arbitrary", "arbitrary", "arbitrary"), vmem_limit_bytes=VMEM_LIMIT),
        name="flash_mla",
    )(qt, qt, k, vt)


def _mlp_kernel(*refs, n_mix, final):
    h_ref = refs[0]
    mix_refs = refs[1:1 + 2 * n_mix]
    gffn_ref, wup_ref, wdown_ref = refs[1 + 2 * n_mix:4 + 2 * n_mix]
    rest = refs[4 + 2 * n_mix:]
    gfin_ref = rest[0] if final else None
    o_ref = rest[-1]

    at = None
    for i in range(n_mix):
        part = _dot(mix_refs[2 * i][...], mix_refs[2 * i + 1][0])
        at = part if at is None else at + part
    h1 = h_ref[0] + at.T
    hn = _rms_rows(h1, gffn_ref[...]).astype(BF16)
    acc = h1
    n_chunks = D_FF // D_MODEL
    for c in range(n_chunks):
        u = jnp.maximum(_dot(hn, wup_ref[:, c * D_MODEL:(c + 1) * D_MODEL]), 0.0)
        acc = acc + _dot((u * u).astype(BF16), wdown_ref[c * D_MODEL:(c + 1) * D_MODEL, :])
    if final:
        acc = _rms_rows(acc, gfin_ref[...])
    o_ref[0] = acc


def _outproj_mlp(h, mixes, gffn, wup, wdown, gfin=None):
    b, s, d = h.shape
    tm = MLP_TM
    args = [h]
    in_specs = [pl.BlockSpec((1, tm, d), lambda i, j: (i, j, 0))]
    for wt, ot in mixes:
        args += [wt, ot]
        in_specs += [_const_spec(wt.shape), pl.BlockSpec((1, ot.shape[1], tm), lambda i, j: (i, 0, j))]
    args += [gffn, wup, wdown]
    in_specs += [_const_spec(gffn.shape), _const_spec(wup.shape), _const_spec(wdown.shape)]
    if gfin is not None:
        args.append(gfin)
        in_specs.append(_const_spec(gfin.shape))
    return pl.pallas_call(
        functools.partial(_mlp_kernel, n_mix=len(mixes), final=gfin is not None),
        grid=(b, s // tm),
        in_specs=in_specs,
        out_specs=pl.BlockSpec((1, tm, d), lambda i, j: (i, j, 0)),
        out_shape=jax.ShapeDtypeStruct((b, s, d), F32),
        compiler_params=pltpu.CompilerParams(
            dimension_semantics=("arbitrary", "arbitrary"), vmem_limit_bytes=VMEM_LIMIT),
        name="outproj_mlp",
    )(*args)


def _odd_proj_kernel(h_ref, g_ref, wqt_ref, wk_ref, wvt_ref, qt_ref, k_ref, vt_ref):
    hn = _rms_rows(h_ref[0], g_ref[...]).astype(BF16)
    scale = HEAD_DIM ** -0.5 * LOG2E
    qt_ref[0] = (_nt_dot(wqt_ref[...], hn) * scale).astype(BF16)
    k_ref[0] = _dot(hn, wk_ref[...]).astype(BF16)
    vt_ref[0] = _nt_dot(wvt_ref[...], hn).astype(BF16)


def _odd_proj(h, g, wqt, wk, wvt):
    b, s, d = h.shape
    tm = PROJ_TM
    c = wqt.shape[0]
    tok = pl.BlockSpec((1, c, tm), lambda i, j: (i, 0, j))
    row = pl.BlockSpec((1, tm, c), lambda i, j: (i, j, 0))
    return pl.pallas_call(
        _odd_proj_kernel,
        grid=(b, s // tm),
        in_specs=[pl.BlockSpec((1, tm, d), lambda i, j: (i, j, 0)), _const_spec(g.shape),
                  _const_spec(wqt.shape), _const_spec(wk.shape), _const_spec(wvt.shape)],
        out_specs=[tok, row, tok],
        out_shape=[jax.ShapeDtypeStruct((b, c, s), BF16),
                   jax.ShapeDtypeStruct((b, s, c), BF16),
                   jax.ShapeDtypeStruct((b, c, s), BF16)],
        compiler_params=pltpu.CompilerParams(
            dimension_semantics=("arbitrary", "arbitrary"), vmem_limit_bytes=VMEM_LIMIT),
        name="odd_proj",
    )(h, g, wqt, wk, wvt)


def _nat_block_geometry(rows):
    nblk = rows // NAT_QROWS
    kinds = []
    for blk in (0, 1, nblk - 1):
        r0 = blk * NAT_QROWS
        ws = min(max(r0 - C_WIN_ROWS // 2, 0), rows - NAT_KROWS)
        table = []
        for kr_l in range(NAT_KROWS):
            line = []
            for qr_l in range(NAT_QROWS):
                qr, kr = r0 + qr_l, ws + kr_l
                rs = min(max(qr - C_WIN_ROWS // 2, 0), rows - C_WIN_ROWS)
                line.append(kr - qr + C_WIN_ROWS - 1 if rs <= kr < rs + C_WIN_ROWS else None)
            table.append(line)
        kinds.append(table)
    return kinds


def _nat_kernel(qt_ref, k_ref, vt_ref, tab_ref, o_ref, bias_ref, sa_ref, sb_ref, mxa_ref, mxb_ref,
                *, rows):
    w = GRID_W
    nq = NAT_QROWS * w
    nblk = rows // NAT_QROWS
    assert nblk >= 4 and nblk % 2 == 0 and rows >= NAT_KROWS
    kinds = _nat_block_geometry(rows)

    @pl.when(pl.program_id(1) == 0)
    def _():
        lane = lax.broadcasted_iota(jnp.int32, (w, 2 * w), 1)
        neg = jnp.full((w, 2 * w), NEG, F32)
        for j in range(2):
            for t, table in enumerate(kinds):
                for kr_l in range(NAT_KROWS):
                    for qp in range(NAT_QROWS // 2):
                        ia, ib = table[kr_l][2 * qp], table[kr_l][2 * qp + 1]
                        ta = neg if ia is None else tab_ref[j, ia]
                        tb = neg if ib is None else tab_ref[j, ib]
                        tile = neg if (ia is None and ib is None) else jnp.where(lane < w, ta, tb)
                        bias_ref[j, t, kr_l * w:(kr_l + 1) * w, qp * 2 * w:(qp + 1) * 2 * w] = tile

    nqp = NAT_QROWS // 2
    lw = 2 * w

    def live(kind, kr_l, qp):
        line = kinds[kind][kr_l]
        return line[2 * qp] is not None or line[2 * qp + 1] is not None

    def key_rows(kind):
        rows_live = [r for r in range(NAT_KROWS) if any(live(kind, r, qp) for qp in range(nqp))]
        lo, hi = rows_live[0], rows_live[-1] + 1
        assert rows_live == list(range(lo, hi))
        hi_pv = lo + 2 * ((hi - lo + 1) // 2)
        assert hi_pv <= NAT_KROWS
        return lo, hi, hi_pv

    def scores(offs, kind, j, s_ref, mx_ref):
        q_off, k_off = offs
        lo, hi, _ = key_rows(kind)
        kw = k_ref[0, pl.ds(k_off + lo * w, (hi - lo) * w), :]
        q = qt_ref[0, j * HEAD_DIM:(j + 1) * HEAD_DIM, pl.ds(q_off, nq)]
        zeros = jnp.zeros_like(q)
        qz = jnp.concatenate([q, zeros] if j == 0 else [zeros, q], axis=0)
        s = _dot(kw, qz)
        for qp in range(nqp):
            cols = slice(qp * lw, (qp + 1) * lw)
            mx = None
            for kr_l in range(lo, hi):
                if live(kind, kr_l, qp):
                    r = slice(kr_l * w, (kr_l + 1) * w)
                    blk = s[(kr_l - lo) * w:(kr_l - lo + 1) * w, cols] + bias_ref[j, kind, r, cols]
                    s_ref[j, r, cols] = blk
                    mx = blk if mx is None else jnp.maximum(mx, blk)
            mx_ref[j, :, cols] = jnp.max(mx, axis=0, keepdims=True)

    def softmax_pv(offs, kind, j, s_ref, mx_ref):
        q_off, k_off = offs
        lo, _, hi_pv = key_rows(kind)
        p_cols = []
        for qp in range(nqp):
            cols = slice(qp * lw, (qp + 1) * lw)
            m = mx_ref[j, :, cols]
            pieces = []
            for kr_l in range(lo, hi_pv):
                if live(kind, kr_l, qp):
                    pieces.append(jnp.exp2(s_ref[j, kr_l * w:(kr_l + 1) * w, cols] - m).astype(BF16))
                else:
                    pieces.append(jnp.zeros((w, lw), BF16))
            p_cols.append(jnp.concatenate(pieces, axis=0))
        p = jnp.concatenate(p_cols, axis=1)
        nkp = (hi_pv - lo) * w
        vw = jnp.concatenate(
            [vt_ref[0, j * HEAD_DIM:(j + 1) * HEAD_DIM, pl.ds(k_off + lo * w, nkp)],
             jnp.ones((NAT_ACC_ROWS - HEAD_DIM, nkp), BF16)], axis=0)
        pv = _dot(vw, p)
        o = pv[:HEAD_DIM] / pv[HEAD_DIM:HEAD_DIM + 1]
        o_ref[0, j * HEAD_DIM:(j + 1) * HEAD_DIM, pl.ds(q_off, nq)] = o.astype(BF16)

    half = (C_WIN_ROWS // 2) * w

    def desc(t):
        if isinstance(t, int):
            if t == 0:
                return (0, 0), 0
            if t == nblk - 1:
                return ((nblk - 1) * nq, (rows - NAT_KROWS) * w), 2
            return (t * nq, t * nq - half), 1
        return (pl.multiple_of(t * nq, nq), pl.multiple_of(t * nq - half, half)), 1

    bufs = ((sa_ref, mxa_ref), (sb_ref, mxb_ref))

    def step(t_next, par_next, t_cur):
        for j in range(2):
            if t_next is not None:
                offs, kind = desc(t_next)
                scores(offs, kind, j, *bufs[par_next])
            if t_cur is not None:
                offs, kind = desc(t_cur)
                softmax_pv(offs, kind, j, *bufs[1 - par_next])

    step(0, 0, None)
    step(1, 1, 0)
    iters = (nblk - 3) // NAT_UNROLL

    def body(i, carry):
        t0 = 2 + NAT_UNROLL * i
        for u in range(NAT_UNROLL):
            step(t0 + u, u % 2, t0 + u - 1)
        return carry

    if iters:
        lax.fori_loop(0, iters, body, 0)
    for t in range(2 + NAT_UNROLL * iters, nblk):
        step(t, t % 2, t - 1)
    step(None, nblk % 2, nblk - 1)


def _natten(qt, k, vt, tab):
    b, c, s = qt.shape
    rows = s // GRID_W
    pairs = C_HEADS // 2
    pc = 2 * HEAD_DIM
    nrel = 2 * C_WIN_ROWS - 1
    return pl.pallas_call(
        functools.partial(_nat_kernel, rows=rows),
        grid=(pairs, b),
        in_specs=[pl.BlockSpec((1, pc, s), lambda h, i: (i, h, 0)),
                  pl.BlockSpec((1, s, pc), lambda h, i: (i, 0, h)),
                  pl.BlockSpec((1, pc, s), lambda h, i: (i, h, 0)),
                  pl.BlockSpec((2, nrel, GRID_W, 2 * GRID_W), lambda h, i: (h, 0, 0, 0))],
        out_specs=pl.BlockSpec((1, pc, s), lambda h, i: (i, h, 0)),
        out_shape=jax.ShapeDtypeStruct((b, c, s), BF16),
        scratch_shapes=[pltpu.VMEM((2, 3, NAT_KROWS * GRID_W, NAT_QROWS * GRID_W), F32),
                        pltpu.VMEM((2, NAT_KROWS * GRID_W, NAT_QROWS * GRID_W), F32),
                        pltpu.VMEM((2, NAT_KROWS * GRID_W, NAT_QROWS * GRID_W), F32),
                        pltpu.VMEM((2, 1, NAT_QROWS * GRID_W), F32),
                        pltpu.VMEM((2, 1, NAT_QROWS * GRID_W), F32)],
        compiler_params=pltpu.CompilerParams(
            dimension_semantics=("arbitrary", "arbitrary"), vmem_limit_bytes=VMEM_LIMIT),
        name="natten",
    )(qt, k, vt, tab)


def _nat_bias_tiles(rpb):
    kc = np.arange(GRID_W)[:, None]
    qc = np.arange(GRID_W)[None, :]
    cs = np.clip(qc - C_WIN_COLS // 2, 0, GRID_W - C_WIN_COLS)
    valid = (kc >= cs) & (kc < cs + C_WIN_COLS)
    rel = kc - qc + (C_WIN_COLS - 1)
    onehot = (rel[None] == np.arange(2 * C_WIN_COLS - 1)[:, None, None]).astype(np.float32)
    tiles = jnp.einsum("had,dkq->hakq", rpb * LOG2E, onehot, precision=lax.Precision.HIGHEST)
    tiles = jnp.where(valid[None, None], tiles, NEG)
    return jnp.concatenate([tiles, tiles], axis=-1).astype(F32)


def _even_layer_weights(w_in, w_uq, w_ukv):
    qa, ka, va, cq, ckv, kr = jnp.split(
        w_in, (A_Q, A_Q + A_KV, A_Q + 2 * A_KV, A_Q + 2 * A_KV + B_Q_RANK,
               A_Q + 2 * A_KV + B_Q_RANK + B_KV_RANK), axis=1)
    w1t = jnp.concatenate([qa, ka, va, cq, kr], axis=1).T.astype(BF16)
    w2 = ckv.astype(BF16)
    uq = w_uq.reshape(B_Q_RANK, B_HEADS, B_NOPE + B_ROPE)
    uq = jnp.pad(uq, ((0, 0), (0, 0), (0, B_QK_PAD - B_NOPE - B_ROPE)))
    wuqt = uq.reshape(B_Q_RANK, B_HEADS * B_QK_PAD).T.astype(BF16)
    ukv = w_ukv.reshape(B_KV_RANK, B_HEADS, B_NOPE + B_V)
    wk = jnp.pad(ukv[:, :, :B_NOPE], ((0, 0), (0, 0), (0, B_QK_PAD - B_NOPE)))
    wk = wk.reshape(B_KV_RANK, B_HEADS * B_QK_PAD).astype(BF16)
    wvt = ukv[:, :, B_NOPE:].reshape(B_KV_RANK, B_HEADS * B_V).T.astype(BF16)
    return w1t, w2, wuqt, wk, wvt


def kernel(x, norm_mix, ev_w_in, ev_a_q_norm, ev_a_k_norm, ev_b_q_norm, ev_b_w_uq, ev_b_kv_norm,
           ev_b_w_ukv, ev_w_out, od_w_qkv, od_rpb, od_w_out, norm_ffn, ffn_w_up, ffn_w_down,
           final_norm):
    depth = norm_mix.shape[0]
    s = x.shape[1]
    ropea = _rope_tables(s, HEAD_DIM)
    ropeb = _rope_tables(s, B_ROPE)
    h = x
    for layer in range(depth):
        i = layer // 2
        gmix = norm_mix[layer][None, :]
        if layer % 2 == 0:
            w1t, w2, wuqt, wk, wvt = _even_layer_weights(ev_w_in[i], ev_b_w_uq[i], ev_b_w_ukv[i])
            qat, ka, vat, qbt, kb, vbt = _even_proj(
                h, gmix, w1t, w2, ev_a_q_norm[i][:, None], ev_a_k_norm[i][:, None],
                ev_b_q_norm[i][:, None], ev_b_kv_norm[i][None, :], wuqt, wk, wvt, ropea, ropeb)
            oat = _flash_gqa(qat, ka, vat)
            obt = _flash_mla(qbt, kb, vbt)
            wot = ev_w_out[i].T.astype(BF16)
            mixes = [(wot[:, :A_Q], oat), (wot[:, A_Q:], obt)]
        else:
            wq, wk_, wv = jnp.split(od_w_qkv[i], 3, axis=1)
            qt, k, vt = _odd_proj(h, gmix, wq.T.astype(BF16), wk_.astype(BF16), wv.T.astype(BF16))
            ot = _natten(qt, k, vt, _nat_bias_tiles(od_rpb[i]))
            mixes = [(od_w_out[i].T.astype(BF16), ot)]
        last = layer == depth - 1
        h = _outproj_mlp(h, mixes, norm_ffn[layer][None, :], ffn_w_up[layer].astype(BF16),
                         ffn_w_down[layer].astype(BF16), final_norm[None, :] if last else None)
    return h
```

```python
import functools
import math

import numpy as np

import jax
import jax.numpy as jnp
from jax import lax
from jax.experimental import pallas as pl
from jax.experimental.pallas import tpu as pltpu

D_MODEL = 1024
GRID_W = 64
HEAD_DIM = 64
ROPE_THETA = 10000.0
EPS = 1e-6
A_HEADS = 8
A_KV_HEADS = 2
B_HEADS = 8
B_Q_RANK = 384
B_KV_RANK = 256
B_NOPE = 64
B_ROPE = 32
B_V = 64
C_HEADS = 16
C_WIN_ROWS = 8
C_WIN_COLS = 16
D_FF = 4 * D_MODEL

A_Q = A_HEADS * HEAD_DIM
A_KV = A_KV_HEADS * HEAD_DIM
LOG2E = math.log2(math.e)
NEG = -1e30

BF16 = jnp.bfloat16
F32 = jnp.float32

V7X_VMEM_BYTES = 64 * 1024 * 1024
V7X_LANES = 128
V7X_BF16_SUBLANES = 16
V7X_MXU_COLS = 256

VMEM_LIMIT = V7X_VMEM_BYTES - 8 * 1024 * 1024

B_QK_PAD = V7X_LANES
PROJ_TM = 1024
MLP_TM = 1024
FLASH_TK = 512
FLASH_M = 1024
FLASH_SLAB = V7X_MXU_COLS
FLASH_UNROLL = 6
ACC_ROWS = HEAD_DIM + V7X_BF16_SUBLANES
NAT_QROWS = 4
NAT_KROWS = NAT_QROWS + C_WIN_ROWS
NAT_UNROLL = 8


def _const_spec(shape):
    nd = len(shape)
    return pl.BlockSpec(shape, lambda *_: (0,) * nd, pipeline_mode=pl.Buffered(1))


def _nt_dot(a, b):
    return lax.dot_general(a, b, (((1,), (1,)), ((), ())), preferred_element_type=F32)


def _dot(a, b):
    return jnp.dot(a, b, preferred_element_type=F32)


def _rms_rows(x, g):
    ms = jnp.mean(x * x, axis=-1, keepdims=True)
    return x * lax.rsqrt(ms + EPS) * g


def _rms_cols(xT, g):
    ms = jnp.mean(xT * xT, axis=0, keepdims=True)
    return xT * lax.rsqrt(ms + EPS) * g


def _rope_cols(xT, cos, sin, bs):
    d = xT.shape[0]
    parts = [xT[i * bs:(i + 1) * bs] for i in range(d // bs)]
    swapped = jnp.concatenate([parts[i ^ 1] for i in range(len(parts))], axis=0)
    return xT * cos + swapped * sin


def _rope_tables(seq, d):
    half = d // 2
    t = jnp.arange(seq, dtype=jnp.int32)
    row = (t // GRID_W).astype(F32)
    col = (t % GRID_W).astype(F32)
    inv = ROPE_THETA ** (-jnp.arange(0, half, 2, dtype=F32) / half)
    ang_r = (row[:, None] * inv[None, :]).T
    ang_c = (col[:, None] * inv[None, :]).T
    cos = jnp.concatenate([jnp.cos(ang_r), jnp.cos(ang_r), jnp.cos(ang_c), jnp.cos(ang_c)], axis=0)
    sin = jnp.concatenate([-jnp.sin(ang_r), jnp.sin(ang_r), -jnp.sin(ang_c), jnp.sin(ang_c)], axis=0)
    return jnp.stack([cos, sin])


def _even_proj_kernel(x_ref, gmix_ref, w1t_ref, w2_ref, gaq_ref, gak_ref, gbq_ref, gbkv_ref,
                      wuqt_ref, wk_ref, wvt_ref, ropea_ref, ropeb_ref,
                      qat_ref, ka_ref, vat_ref, qbt_ref, kb_ref, vbt_ref):
    x = x_ref[0]
    hn = _rms_rows(x, gmix_ref[...]).astype(BF16)
    zt = _nt_dot(w1t_ref[...], hn)
    z2 = _dot(hn, w2_ref[...])

    cos_a, sin_a = ropea_ref[0], ropea_ref[1]
    cos_b, sin_b = ropeb_ref[0], ropeb_ref[1]
    scale_a = HEAD_DIM ** -0.5 * LOG2E
    scale_b = (B_NOPE + B_ROPE) ** -0.5 * LOG2E

    gaq = gaq_ref[...]
    for h in range(A_HEADS):
        blk = zt[h * HEAD_DIM:(h + 1) * HEAD_DIM]
        q = _rope_cols(_rms_cols(blk, gaq), cos_a, sin_a, HEAD_DIM // 4) * scale_a
        qat_ref[0, h * HEAD_DIM:(h + 1) * HEAD_DIM, :] = q.astype(BF16)

    gak = gak_ref[...]
    kts = []
    for h in range(A_KV_HEADS):
        blk = zt[A_Q + h * HEAD_DIM:A_Q + (h + 1) * HEAD_DIM]
        kts.append(_rope_cols(_rms_cols(blk, gak), cos_a, sin_a, HEAD_DIM // 4))
    ka_ref[0] = jnp.concatenate(kts, axis=0).T.astype(BF16)

    vat_ref[0] = zt[A_Q + A_KV:A_Q + 2 * A_KV].astype(BF16)

    c0 = A_Q + 2 * A_KV
    cqn = _rms_cols(zt[c0:c0 + B_Q_RANK], gbq_ref[...]).astype(BF16)
    qbt = _dot(wuqt_ref[...], cqn)
    for h in range(B_HEADS):
        blk = qbt[h * B_QK_PAD:(h + 1) * B_QK_PAD]
        roped = _rope_cols(blk[B_NOPE:B_NOPE + B_ROPE], cos_b, sin_b, B_ROPE // 4)
        q = jnp.concatenate([blk[:B_NOPE], roped, blk[B_NOPE + B_ROPE:]], axis=0) * scale_b
        qbt_ref[0, h * B_QK_PAD:(h + 1) * B_QK_PAD, :] = q.astype(BF16)

    k0 = c0 + B_Q_RANK
    kr = _rope_cols(zt[k0:k0 + B_ROPE], cos_b, sin_b, B_ROPE // 4)
    tm = kr.shape[1]
    krp = jnp.concatenate([jnp.zeros((B_NOPE, tm), F32), kr,
                           jnp.zeros((B_QK_PAD - B_NOPE - B_ROPE, tm), F32)], axis=0).T

    kvn = _rms_rows(z2, gbkv_ref[...]).astype(BF16)
    kn = _dot(kvn, wk_ref[...])
    for h in range(B_HEADS):
        kb_ref[0, :, h * B_QK_PAD:(h + 1) * B_QK_PAD] = (
            kn[:, h * B_QK_PAD:(h + 1) * B_QK_PAD] + krp).astype(BF16)
    vbt_ref[0] = _nt_dot(wvt_ref[...], kvn).astype(BF16)


def _even_proj(x, gmix, w1t, w2, gaq, gak, gbq, gbkv, wuqt, wk, wvt, ropea, ropeb):
    b, s, d = x.shape
    tm = PROJ_TM
    tok = lambda c: pl.BlockSpec((1, c, tm), lambda i, j: (i, 0, j))
    row = lambda c: pl.BlockSpec((1, tm, c), lambda i, j: (i, j, 0))
    return pl.pallas_call(
        _even_proj_kernel,
        grid=(b, s // tm),
        in_specs=[row(d), _const_spec(gmix.shape), _const_spec(w1t.shape), _const_spec(w2.shape),
                  _const_spec(gaq.shape), _const_spec(gak.shape), _const_spec(gbq.shape),
                  _const_spec(gbkv.shape), _const_spec(wuqt.shape), _const_spec(wk.shape),
                  _const_spec(wvt.shape),
                  pl.BlockSpec((2, HEAD_DIM, tm), lambda i, j: (0, 0, j)),
                  pl.BlockSpec((2, B_ROPE, tm), lambda i, j: (0, 0, j))],
        out_specs=[tok(A_Q), row(A_KV), tok(A_KV), tok(B_HEADS * B_QK_PAD),
                   row(B_HEADS * B_QK_PAD), tok(B_HEADS * B_V)],
        out_shape=[jax.ShapeDtypeStruct((b, A_Q, s), BF16),
                   jax.ShapeDtypeStruct((b, s, A_KV), BF16),
                   jax.ShapeDtypeStruct((b, A_KV, s), BF16),
                   jax.ShapeDtypeStruct((b, B_HEADS * B_QK_PAD, s), BF16),
                   jax.ShapeDtypeStruct((b, s, B_HEADS * B_QK_PAD), BF16),
                   jax.ShapeDtypeStruct((b, B_HEADS * B_V, s), BF16)],
        compiler_params=pltpu.CompilerParams(
            dimension_semantics=("arbitrary", "arbitrary"), vmem_limit_bytes=VMEM_LIMIT),
        name="even_proj",
    )(x, gmix, w1t, w2, gaq, gak, gbq, gbkv, wuqt, wk, wvt, ropea, ropeb)


def _flash_kernel(qt_ref, qtn_ref, k_ref, vt_ref, o_ref, qz_ref, m_ref, acc_ref,
                  sa_ref, sb_ref, mxa_ref, mxb_ref, *, gqa, tk, q_axis):
    seq = k_ref.shape[1]
    tq = qt_ref.shape[2]
    m_cols = qz_ref.shape[3]
    dv = HEAD_DIM
    n = seq // tk
    assert n % 2 == 0
    jq = pl.program_id(q_axis)
    slot = jq % 2

    def stage(src_ref, dst_slot):
        for st in range(2):
            if gqa:
                g = m_cols // tq
                qcat = jnp.concatenate(
                    [src_ref[0, (st * g + i) * HEAD_DIM:(st * g + i + 1) * HEAD_DIM, :] for i in range(g)],
                    axis=1)
                zeros = jnp.zeros_like(qcat)
                qz_ref[dst_slot, st] = jnp.concatenate([qcat, zeros] if st == 0 else [zeros, qcat], axis=0)
            else:
                qz_ref[dst_slot, st] = src_ref[0, st * B_QK_PAD:(st + 1) * B_QK_PAD, :]

    slabs = [(st, sl * FLASH_SLAB) for st in range(2) for sl in range(m_cols // FLASH_SLAB)]

    def k_chunk(c, st):
        off = pl.multiple_of(c * tk, tk)
        if gqa:
            return k_ref[0, pl.ds(off, tk), :]
        return k_ref[0, pl.ds(off, tk), st * B_QK_PAD:(st + 1) * B_QK_PAD]

    ones = jnp.ones((ACC_ROWS - dv, tk), BF16)

    def v_chunk(c, st):
        off = pl.multiple_of(c * tk, tk)
        return jnp.concatenate([vt_ref[0, st * dv:(st + 1) * dv, pl.ds(off, tk)], ones], axis=0)

    def scores(kc, q_slot, st, col, s_ref, mx_ref):
        cols = slice(col, col + FLASH_SLAB)
        s = _dot(kc, qz_ref[q_slot, st, :, cols])
        s_ref[st, col // FLASH_SLAB] = s
        mx_ref[st, :, cols] = jnp.max(s, axis=0, keepdims=True)

    def softmax_pv(vc, st, col, s_ref, mx_ref):
        cols = slice(col, col + FLASH_SLAB)
        m_prev = m_ref[st, :, cols]
        m_new = jnp.maximum(m_prev, mx_ref[st, :, cols])
        alpha = jnp.exp2(m_prev - m_new)
        p = jnp.exp2(s_ref[st, col // FLASH_SLAB] - m_new).astype(BF16)
        acc_ref[st, :, cols] = alpha * acc_ref[st, :, cols] + _dot(vc, p)
        m_ref[st, :, cols] = m_new

    def step(c_next, q_slot, nxt, c_cur, cur):
        kcs = [k_chunk(c_next, st) for st in range(2)] if nxt is not None else None
        vcs = [v_chunk(c_cur, st) for st in range(2)] if cur is not None else None
        for st, col in slabs:
            if nxt is not None:
                scores(kcs[st], q_slot, st, col, *nxt)
            if cur is not None:
                softmax_pv(vcs[st], st, col, *cur)

    bufs = ((sa_ref, mxa_ref), (sb_ref, mxb_ref))

    @pl.when(jq == 0)
    def _():
        stage(qt_ref, 0)
        step(0, 0, bufs[0], None, None)

    stage(qtn_ref, 1 - slot)
    m_ref[...] = jnp.full(m_ref.shape, NEG, F32)
    acc_ref[...] = jnp.zeros(acc_ref.shape, F32)
    iters = (n - 1) // FLASH_UNROLL

    def body(i, carry):
        c0 = 1 + FLASH_UNROLL * i
        for u in range(FLASH_UNROLL):
            step(c0 + u, slot, bufs[(1 + u) % 2], c0 + u - 1, bufs[u % 2])
        return carry

    if iters:
        lax.fori_loop(0, iters, body, 0)
    for c in range(1 + FLASH_UNROLL * iters, n):
        step(c, slot, bufs[c % 2], c - 1, bufs[(c - 1) % 2])
    step(0, 1 - slot, bufs[0], n - 1, bufs[(n - 1) % 2])

    for st in range(2):
        acc = acc_ref[st]
        o = acc[:dv] / acc[dv:dv + 1]
        if gqa:
            g = m_cols // tq
            for i in range(g):
                o_ref[0, (st * g + i) * dv:(st * g + i + 1) * dv, :] = o[:, i * tq:(i + 1) * tq].astype(BF16)
        else:
            o_ref[0, st * dv:(st + 1) * dv, :] = o.astype(BF16)


def _flash_scratch(dk):
    return [pltpu.VMEM((2, 2, dk, FLASH_M), BF16),
            pltpu.VMEM((2, 1, FLASH_M), F32),
            pltpu.VMEM((2, ACC_ROWS, FLASH_M), F32),
            pltpu.VMEM((2, FLASH_M // FLASH_SLAB, FLASH_TK, FLASH_SLAB), F32),
            pltpu.VMEM((2, FLASH_M // FLASH_SLAB, FLASH_TK, FLASH_SLAB), F32),
            pltpu.VMEM((2, 1, FLASH_M), F32),
            pltpu.VMEM((2, 1, FLASH_M), F32)]


def _flash_gqa(qt, k, vt):
    b, cq, s = qt.shape
    g = A_HEADS // A_KV_HEADS
    tq = FLASH_M // g
    return pl.pallas_call(
        functools.partial(_flash_kernel, gqa=True, tk=FLASH_TK, q_axis=1),
        grid=(b, s // tq),
        in_specs=[pl.BlockSpec((1, cq, tq), lambda i, j: (i, 0, j)),
                  pl.BlockSpec((1, cq, tq), lambda i, j: (i, 0, jnp.minimum(j + 1, s // tq - 1))),
                  pl.BlockSpec((1, s, A_KV), lambda i, j: (i, 0, 0)),
                  pl.BlockSpec((1, A_KV, s), lambda i, j: (i, 0, 0))],
        out_specs=pl.BlockSpec((1, cq, tq), lambda i, j: (i, 0, j)),
        out_shape=jax.ShapeDtypeStruct((b, cq, s), BF16),
        scratch_shapes=_flash_scratch(2 * HEAD_DIM),
        compiler_params=pltpu.CompilerParams(
            dimension_semantics=("arbitrary", "arbitrary"), vmem_limit_bytes=VMEM_LIMIT),
        name="flash_gqa",
    )(qt, qt, k, vt)


def _flash_mla(qt, k, vt):
    b, _, s = qt.shape
    tq = FLASH_M
    pairs = B_HEADS // 2
    return pl.pallas_call(
        functools.partial(_flash_kernel, gqa=False, tk=FLASH_TK, q_axis=2),
        grid=(b, pairs, s // tq),
        in_specs=[pl.BlockSpec((1, 2 * B_QK_PAD, tq), lambda i, h, j: (i, h, j)),
                  pl.BlockSpec((1, 2 * B_QK_PAD, tq), lambda i, h, j: (i, h, jnp.minimum(j + 1, s // tq - 1))),
                  pl.BlockSpec((1, s, 2 * B_QK_PAD), lambda i, h, j: (i, 0, h)),
                  pl.BlockSpec((1, 2 * B_V, s), lambda i, h, j: (i, h, 0))],
        out_specs=pl.BlockSpec((1, 2 * B_V, tq), lambda i, h, j: (i, h, j)),
        out_shape=jax.ShapeDtypeStruct((b, B_HEADS * B_V, s), BF16),
        scratch_shapes=_flash_scratch(B_QK_PAD),
        compiler_params=pltpu.CompilerParams(
            dimension_semantics=("arbitrary", "arbitrary", "arbitrary"), vmem_limit_bytes=VMEM_LIMIT),
        name="flash_mla",
    )(qt, qt, k, vt)


def _mlp_kernel(*refs, n_mix, final):
    h_ref = refs[0]
    mix_refs = refs[1:1 + 2 * n_mix]
    gffn_ref, wup_ref, wdown_ref = refs[1 + 2 * n_mix:4 + 2 * n_mix]
    rest = refs[4 + 2 * n_mix:]
    gfin_ref = rest[0] if final else None
    o_ref = rest[-1]

    at = None
    for i in range(n_mix):
        part = _dot(mix_refs[2 * i][...], mix_refs[2 * i + 1][0])
        at = part if at is None else at + part
    h1 = h_ref[0] + at.T
    hn = _rms_rows(h1, gffn_ref[...]).astype(BF16)
    acc = h1
    n_chunks = D_FF // D_MODEL
    for c in range(n_chunks):
        u = jnp.maximum(_dot(hn, wup_ref[:, c * D_MODEL:(c + 1) * D_MODEL]), 0.0)
        acc = acc + _dot((u * u).astype(BF16), wdown_ref[c * D_MODEL:(c + 1) * D_MODEL, :])
    if final:
        acc = _rms_rows(acc, gfin_ref[...])
    o_ref[0] = acc


def _outproj_mlp(h, mixes, gffn, wup, wdown, gfin=None):
    b, s, d = h.shape
    tm = MLP_TM
    args = [h]
    in_specs = [pl.BlockSpec((1, tm, d), lambda i, j: (i, j, 0))]
    for wt, ot in mixes:
        args += [wt, ot]
        in_specs += [_const_spec(wt.shape), pl.BlockSpec((1, ot.shape[1], tm), lambda i, j: (i, 0, j))]
    args += [gffn, wup, wdown]
    in_specs += [_const_spec(gffn.shape), _const_spec(wup.shape), _const_spec(wdown.shape)]
    if gfin is not None:
        args.append(gfin)
        in_specs.append(_const_spec(gfin.shape))
    return pl.pallas_call(
        functools.partial(_mlp_kernel, n_mix=len(mixes), final=gfin is not None),
        grid=(b, s // tm),
        in_specs=in_specs,
        out_specs=pl.BlockSpec((1, tm, d), lambda i, j: (i, j, 0)),
        out_shape=jax.ShapeDtypeStruct((b, s, d), F32),
        compiler_params=pltpu.CompilerParams(
            dimension_semantics=("arbitrary", "arbitrary"), vmem_limit_bytes=VMEM_LIMIT),
        name="outproj_mlp",
    )(*args)


def _odd_proj_kernel(h_ref, g_ref, wqt_ref, wk_ref, wvt_ref, qt_ref, k_ref, vt_ref):
    hn = _rms_rows(h_ref[0], g_ref[...]).astype(BF16)
    scale = HEAD_DIM ** -0.5 * LOG2E
    qt_ref[0] = (_nt_dot(wqt_ref[...], hn) * scale).astype(BF16)
    k_ref[0] = _dot(hn, wk_ref[...]).astype(BF16)
    vt_ref[0] = _nt_dot(wvt_ref[...], hn).astype(BF16)


def _odd_proj(h, g, wqt, wk, wvt):
    b, s, d = h.shape
    tm = PROJ_TM
    c = wqt.shape[0]
    tok = pl.BlockSpec((1, c, tm), lambda i, j: (i, 0, j))
    row = pl.BlockSpec((1, tm, c), lambda i, j: (i, j, 0))
    return pl.pallas_call(
        _odd_proj_kernel,
        grid=(b, s // tm),
        in_specs=[pl.BlockSpec((1, tm, d), lambda i, j: (i, j, 0)), _const_spec(g.shape),
                  _const_spec(wqt.shape), _const_spec(wk.shape), _const_spec(wvt.shape)],
        out_specs=[tok, row, tok],
        out_shape=[jax.ShapeDtypeStruct((b, c, s), BF16),
                   jax.ShapeDtypeStruct((b, s, c), BF16),
                   jax.ShapeDtypeStruct((b, c, s), BF16)],
        compiler_params=pltpu.CompilerParams(
            dimension_semantics=("arbitrary", "arbitrary"), vmem_limit_bytes=VMEM_LIMIT),
        name="odd_proj",
    )(h, g, wqt, wk, wvt)


def _nat_block_geometry(rows):
    nblk = rows // NAT_QROWS
    kinds = []
    for blk in (0, 1, nblk - 1):
        r0 = blk * NAT_QROWS
        ws = min(max(r0 - C_WIN_ROWS // 2, 0), rows - NAT_KROWS)
        table = []
        for kr_l in range(NAT_KROWS):
            line = []
            for qr_l in range(NAT_QROWS):
                qr, kr = r0 + qr_l, ws + kr_l
                rs = min(max(qr - C_WIN_ROWS // 2, 0), rows - C_WIN_ROWS)
                line.append(kr - qr + C_WIN_ROWS - 1 if rs <= kr < rs + C_WIN_ROWS else None)
            table.append(line)
        kinds.append(table)
    return kinds


def _nat_kernel(qt_ref, k_ref, vt_ref, tab_ref, o_ref, bias_ref, sa_ref, sb_ref, mxa_ref, mxb_ref,
                *, rows):
    w = GRID_W
    nq = NAT_QROWS * w
    nblk = rows // NAT_QROWS
    assert nblk >= 4 and nblk % 2 == 0 and rows >= NAT_KROWS
    kinds = _nat_block_geometry(rows)

    @pl.when(pl.program_id(1) == 0)
    def _():
        lane = lax.broadcasted_iota(jnp.int32, (w, 2 * w), 1)
        neg = jnp.full((w, 2 * w), NEG, F32)
        for j in range(2):
            for t, table in enumerate(kinds):
                for kr_l in range(NAT_KROWS):
                    for qp in range(NAT_QROWS // 2):
                        ia, ib = table[kr_l][2 * qp], table[kr_l][2 * qp + 1]
                        ta = neg if ia is None else tab_ref[j, ia]
                        tb = neg if ib is None else tab_ref[j, ib]
                        tile = neg if (ia is None and ib is None) else jnp.where(lane < w, ta, tb)
                        bias_ref[j, t, kr_l * w:(kr_l + 1) * w, qp * 2 * w:(qp + 1) * 2 * w] = tile

    nqp = NAT_QROWS // 2
    lw = 2 * w
    assert lw == V7X_LANES

    def live(kind, kr_l, qp):
        line = kinds[kind][kr_l]
        return line[2 * qp] is not None or line[2 * qp + 1] is not None

    def key_rows(kind):
        rows_live = [r for r in range(NAT_KROWS) if any(live(kind, r, qp) for qp in range(nqp))]
        lo, hi = rows_live[0], rows_live[-1] + 1
        assert rows_live == list(range(lo, hi))
        hi_pv = lo + 2 * ((hi - lo + 1) // 2)
        assert hi_pv <= NAT_KROWS
        return lo, hi, hi_pv

    def scores(offs, kind, j, s_ref, mx_ref):
        q_off, k_off = offs
        lo, hi, _ = key_rows(kind)
        kw = k_ref[0, pl.ds(k_off + lo * w, (hi - lo) * w), :]
        q = qt_ref[0, j * HEAD_DIM:(j + 1) * HEAD_DIM, pl.ds(q_off, nq)]
        zeros = jnp.zeros_like(q)
        qz = jnp.concatenate([q, zeros] if j == 0 else [zeros, q], axis=0)
        s = _dot(kw, qz)
        for qp in range(nqp):
            cols = slice(qp * lw, (qp + 1) * lw)
            mx = None
            for kr_l in range(lo, hi):
                if live(kind, kr_l, qp):
                    r = slice(kr_l * w, (kr_l + 1) * w)
                    blk = s[(kr_l - lo) * w:(kr_l - lo + 1) * w, cols] + bias_ref[j, kind, r, cols]
                    s_ref[j, r, cols] = blk
                    mx = blk if mx is None else jnp.maximum(mx, blk)
            mx_ref[j, :, cols] = jnp.max(mx, axis=0, keepdims=True)

    def softmax_pv(offs, kind, j, s_ref, mx_ref):
        q_off, k_off = offs
        lo, _, hi_pv = key_rows(kind)
        p_cols = []
        for qp in range(nqp):
            cols = slice(qp * lw, (qp + 1) * lw)
            m = mx_ref[j, :, cols]
            pieces = []
            for kr_l in range(lo, hi_pv):
                if live(kind, kr_l, qp):
                    pieces.append(jnp.exp2(s_ref[j, kr_l * w:(kr_l + 1) * w, cols] - m).astype(BF16))
                else:
                    pieces.append(jnp.zeros((w, lw), BF16))
            p_cols.append(jnp.concatenate(pieces, axis=0))
        p = jnp.concatenate(p_cols, axis=1)
        nkp = (hi_pv - lo) * w
        vw = jnp.concatenate(
            [vt_ref[0, j * HEAD_DIM:(j + 1) * HEAD_DIM, pl.ds(k_off + lo * w, nkp)],
             jnp.ones((ACC_ROWS - HEAD_DIM, nkp), BF16)], axis=0)
        pv = _dot(vw, p)
        o = pv[:HEAD_DIM] / pv[HEAD_DIM:HEAD_DIM + 1]
        o_ref[0, j * HEAD_DIM:(j + 1) * HEAD_DIM, pl.ds(q_off, nq)] = o.astype(BF16)

    half = (C_WIN_ROWS // 2) * w

    def desc(t):
        if isinstance(t, int):
            if t == 0:
                return (0, 0), 0
            if t == nblk - 1:
                return ((nblk - 1) * nq, (rows - NAT_KROWS) * w), 2
            return (t * nq, t * nq - half), 1
        return (pl.multiple_of(t * nq, nq), pl.multiple_of(t * nq - half, half)), 1

    bufs = ((sa_ref, mxa_ref), (sb_ref, mxb_ref))

    def step(t_next, par_next, t_cur):
        for j in range(2):
            if t_next is not None:
                offs, kind = desc(t_next)
                scores(offs, kind, j, *bufs[par_next])
            if t_cur is not None:
                offs, kind = desc(t_cur)
                softmax_pv(offs, kind, j, *bufs[1 - par_next])

    step(0, 0, None)
    step(1, 1, 0)
    iters = (nblk - 3) // NAT_UNROLL

    def body(i, carry):
        t0 = 2 + NAT_UNROLL * i
        for u in range(NAT_UNROLL):
            step(t0 + u, u % 2, t0 + u - 1)
        return carry

    if iters:
        lax.fori_loop(0, iters, body, 0)
    for t in range(2 + NAT_UNROLL * iters, nblk):
        step(t, t % 2, t - 1)
    step(None, nblk % 2, nblk - 1)


def _natten(qt, k, vt, tab):
    b, c, s = qt.shape
    rows = s // GRID_W
    pairs = C_HEADS // 2
    pc = 2 * HEAD_DIM
    nrel = 2 * C_WIN_ROWS - 1
    return pl.pallas_call(
        functools.partial(_nat_kernel, rows=rows),
        grid=(pairs, b),
        in_specs=[pl.BlockSpec((1, pc, s), lambda h, i: (i, h, 0)),
                  pl.BlockSpec((1, s, pc), lambda h, i: (i, 0, h)),
                  pl.BlockSpec((1, pc, s), lambda h, i: (i, h, 0)),
                  pl.BlockSpec((2, nrel, GRID_W, 2 * GRID_W), lambda h, i: (h, 0, 0, 0))],
        out_specs=pl.BlockSpec((1, pc, s), lambda h, i: (i, h, 0)),
        out_shape=jax.ShapeDtypeStruct((b, c, s), BF16),
        scratch_shapes=[pltpu.VMEM((2, 3, NAT_KROWS * GRID_W, NAT_QROWS * GRID_W), F32),
                        pltpu.VMEM((2, NAT_KROWS * GRID_W, NAT_QROWS * GRID_W), F32),
                        pltpu.VMEM((2, NAT_KROWS * GRID_W, NAT_QROWS * GRID_W), F32),
                        pltpu.VMEM((2, 1, NAT_QROWS * GRID_W), F32),
                        pltpu.VMEM((2, 1, NAT_QROWS * GRID_W), F32)],
        compiler_params=pltpu.CompilerParams(
            dimension_semantics=("arbitrary", "arbitrary"), vmem_limit_bytes=VMEM_LIMIT),
        name="natten",
    )(qt, k, vt, tab)


def _nat_bias_tiles(rpb):
    kc = np.arange(GRID_W)[:, None]
    qc = np.arange(GRID_W)[None, :]
    cs = np.clip(qc - C_WIN_COLS // 2, 0, GRID_W - C_WIN_COLS)
    valid = (kc >= cs) & (kc < cs + C_WIN_COLS)
    rel = kc - qc + (C_WIN_COLS - 1)
    onehot = (rel[None] == np.arange(2 * C_WIN_COLS - 1)[:, None, None]).astype(np.float32)
    tiles = jnp.einsum("had,dkq->hakq", rpb * LOG2E, onehot, precision=lax.Precision.HIGHEST)
    tiles = jnp.where(valid[None, None], tiles, NEG)
    return jnp.concatenate([tiles, tiles], axis=-1).astype(F32)


def _even_layer_weights(w_in, w_uq, w_ukv):
    qa, ka, va, cq, ckv, kr = jnp.split(
        w_in, (A_Q, A_Q + A_KV, A_Q + 2 * A_KV, A_Q + 2 * A_KV + B_Q_RANK,
               A_Q + 2 * A_KV + B_Q_RANK + B_KV_RANK), axis=1)
    w1t = jnp.concatenate([qa, ka, va, cq, kr], axis=1).T.astype(BF16)
    w2 = ckv.astype(BF16)
    uq = w_uq.reshape(B_Q_RANK, B_HEADS, B_NOPE + B_ROPE)
    uq = jnp.pad(uq, ((0, 0), (0, 0), (0, B_QK_PAD - B_NOPE - B_ROPE)))
    wuqt = uq.reshape(B_Q_RANK, B_HEADS * B_QK_PAD).T.astype(BF16)
    ukv = w_ukv.reshape(B_KV_RANK, B_HEADS, B_NOPE + B_V)
    wk = jnp.pad(ukv[:, :, :B_NOPE], ((0, 0), (0, 0), (0, B_QK_PAD - B_NOPE)))
    wk = wk.reshape(B_KV_RANK, B_HEADS * B_QK_PAD).astype(BF16)
    wvt = ukv[:, :, B_NOPE:].reshape(B_KV_RANK, B_HEADS * B_V).T.astype(BF16)
    return w1t, w2, wuqt, wk, wvt


def kernel(x, norm_mix, ev_w_in, ev_a_q_norm, ev_a_k_norm, ev_b_q_norm, ev_b_w_uq, ev_b_kv_norm,
           ev_b_w_ukv, ev_w_out, od_w_qkv, od_rpb, od_w_out, norm_ffn, ffn_w_up, ffn_w_down,
           final_norm):
    depth = norm_mix.shape[0]
    s = x.shape[1]
    ropea = _rope_tables(s, HEAD_DIM)
    ropeb = _rope_tables(s, B_ROPE)
    h = x
    for layer in range(depth):
        i = layer // 2
        gmix = norm_mix[layer][None, :]
        if layer % 2 == 0:
            w1t, w2, wuqt, wk, wvt = _even_layer_weights(ev_w_in[i], ev_b_w_uq[i], ev_b_w_ukv[i])
            qat, ka, vat, qbt, kb, vbt = _even_proj(
                h, gmix, w1t, w2, ev_a_q_norm[i][:, None], ev_a_k_norm[i][:, None],
                ev_b_q_norm[i][:, None], ev_b_kv_norm[i][None, :], wuqt, wk, wvt, ropea, ropeb)
            oat = _flash_gqa(qat, ka, vat)
            obt = _flash_mla(qbt, kb, vbt)
            wot = ev_w_out[i].T.astype(BF16)
            mixes = [(wot[:, :A_Q], oat), (wot[:, A_Q:], obt)]
        else:
            wq, wk_, wv = jnp.split(od_w_qkv[i], 3, axis=1)
            qt, k, vt = _odd_proj(h, gmix, wq.T.astype(BF16), wk_.astype(BF16), wv.T.astype(BF16))
            ot = _natten(qt, k, vt, _nat_bias_tiles(od_rpb[i]))
            mixes = [(od_w_out[i].T.astype(BF16), ot)]
        last = layer == depth - 1
        h = _outproj_mlp(h, mixes, norm_ffn[layer][None, :], ffn_w_up[layer].astype(BF16),
                         ffn_w_down[layer].astype(BF16), final_norm[None, :] if last else None)
    return h
```

```python
import functools
import math

import numpy as np

import jax
import jax.numpy as jnp
from jax import lax
from jax.experimental import pallas as pl
from jax.experimental.pallas import tpu as pltpu

D_MODEL = 1024
GRID_W = 64
HEAD_DIM = 64
ROPE_THETA = 10000.0
EPS = 1e-6
A_HEADS = 8
A_KV_HEADS = 2
B_HEADS = 8
B_Q_RANK = 384
B_KV_RANK = 256
B_NOPE = 64
B_ROPE = 32
B_V = 64
C_HEADS = 16
C_WIN_ROWS = 8
C_WIN_COLS = 16
D_FF = 4 * D_MODEL

A_Q = A_HEADS * HEAD_DIM
A_KV = A_KV_HEADS * HEAD_DIM
LOG2E = math.log2(math.e)
NEG = -1e30

BF16 = jnp.bfloat16
F32 = jnp.float32

V7X_VMEM_BYTES = 64 * 1024 * 1024
V7X_LANES = 128
V7X_BF16_SUBLANES = 16
V7X_MXU_COLS = 256

VMEM_LIMIT = V7X_VMEM_BYTES - 8 * 1024 * 1024

B_QK_PAD = V7X_LANES
PROJ_TM = 1024
MLP_TM = 1024
FLASH_TK = 512
FLASH_M = 1024
FLASH_SLAB = V7X_MXU_COLS
FLASH_UNROLL = 6
ACC_ROWS = HEAD_DIM + V7X_BF16_SUBLANES
NAT_QROWS = 4
NAT_KROWS = NAT_QROWS + C_WIN_ROWS
NAT_UNROLL = 8


def _const_spec(shape):
    nd = len(shape)
    return pl.BlockSpec(shape, lambda *_: (0,) * nd, pipeline_mode=pl.Buffered(1))


def _nt_dot(a, b):
    return lax.dot_general(a, b, (((1,), (1,)), ((), ())), preferred_element_type=F32)


def _dot(a, b):
    return jnp.dot(a, b, preferred_element_type=F32)


def _rms_rows(x, g):
    ms = jnp.mean(x * x, axis=-1, keepdims=True)
    return x * lax.rsqrt(ms + EPS) * g


def _rms_cols(xT, g):
    ms = jnp.mean(xT * xT, axis=0, keepdims=True)
    return xT * lax.rsqrt(ms + EPS) * g


def _rope_cols(xT, cos, sin, bs):
    d = xT.shape[0]
    parts = [xT[i * bs:(i + 1) * bs] for i in range(d // bs)]
    swapped = jnp.concatenate([parts[i ^ 1] for i in range(len(parts))], axis=0)
    return xT * cos + swapped * sin


def _rope_tables(seq, d):
    half = d // 2
    t = jnp.arange(seq, dtype=jnp.int32)
    row = (t // GRID_W).astype(F32)
    col = (t % GRID_W).astype(F32)
    inv = ROPE_THETA ** (-jnp.arange(0, half, 2, dtype=F32) / half)
    ang_r = (row[:, None] * inv[None, :]).T
    ang_c = (col[:, None] * inv[None, :]).T
    cos = jnp.concatenate([jnp.cos(ang_r), jnp.cos(ang_r), jnp.cos(ang_c), jnp.cos(ang_c)], axis=0)
    sin = jnp.concatenate([-jnp.sin(ang_r), jnp.sin(ang_r), -jnp.sin(ang_c), jnp.sin(ang_c)], axis=0)
    return jnp.stack([cos, sin])


def _even_proj_kernel(x_ref, gmix_ref, w1t_ref, w2_ref, gaq_ref, gak_ref, gbq_ref, gbkv_ref,
                      wuqt_ref, wk_ref, wvt_ref, ropea_ref, ropeb_ref,
                      qat_ref, ka_ref, vat_ref, qbt_ref, kb_ref, vbt_ref):
    x = x_ref[0]
    hn = _rms_rows(x, gmix_ref[...]).astype(BF16)
    zt = _nt_dot(w1t_ref[...], hn)
    z2 = _dot(hn, w2_ref[...])

    cos_a, sin_a = ropea_ref[0], ropea_ref[1]
    cos_b, sin_b = ropeb_ref[0], ropeb_ref[1]
    scale_a = HEAD_DIM ** -0.5 * LOG2E
    scale_b = (B_NOPE + B_ROPE) ** -0.5 * LOG2E

    gaq = gaq_ref[...]
    for h in range(A_HEADS):
        blk = zt[h * HEAD_DIM:(h + 1) * HEAD_DIM]
        q = _rope_cols(_rms_cols(blk, gaq), cos_a, sin_a, HEAD_DIM // 4) * scale_a
        qat_ref[0, h * HEAD_DIM:(h + 1) * HEAD_DIM, :] = q.astype(BF16)

    gak = gak_ref[...]
    kts = []
    for h in range(A_KV_HEADS):
        blk = zt[A_Q + h * HEAD_DIM:A_Q + (h + 1) * HEAD_DIM]
        kts.append(_rope_cols(_rms_cols(blk, gak), cos_a, sin_a, HEAD_DIM // 4))
    ka_ref[0] = jnp.concatenate(kts, axis=0).T.astype(BF16)

    vat_ref[0] = zt[A_Q + A_KV:A_Q + 2 * A_KV].astype(BF16)

    c0 = A_Q + 2 * A_KV
    cqn = _rms_cols(zt[c0:c0 + B_Q_RANK], gbq_ref[...]).astype(BF16)
    qbt = _dot(wuqt_ref[...], cqn)
    for h in range(B_HEADS):
        blk = qbt[h * B_QK_PAD:(h + 1) * B_QK_PAD]
        roped = _rope_cols(blk[B_NOPE:B_NOPE + B_ROPE], cos_b, sin_b, B_ROPE // 4)
        q = jnp.concatenate([blk[:B_NOPE], roped, blk[B_NOPE + B_ROPE:]], axis=0) * scale_b
        qbt_ref[0, h * B_QK_PAD:(h + 1) * B_QK_PAD, :] = q.astype(BF16)

    k0 = c0 + B_Q_RANK
    kr = _rope_cols(zt[k0:k0 + B_ROPE], cos_b, sin_b, B_ROPE // 4)
    tm = kr.shape[1]
    krp = jnp.concatenate([jnp.zeros((B_NOPE, tm), F32), kr,
                           jnp.zeros((B_QK_PAD - B_NOPE - B_ROPE, tm), F32)], axis=0).T

    kvn = _rms_rows(z2, gbkv_ref[...]).astype(BF16)
    kn = _dot(kvn, wk_ref[...])
    for h in range(B_HEADS):
        kb_ref[0, :, h * B_QK_PAD:(h + 1) * B_QK_PAD] = (
            kn[:, h * B_QK_PAD:(h + 1) * B_QK_PAD] + krp).astype(BF16)
    vbt_ref[0] = _nt_dot(wvt_ref[...], kvn).astype(BF16)


def _even_proj(x, gmix, w1t, w2, gaq, gak, gbq, gbkv, wuqt, wk, wvt, ropea, ropeb):
    b, s, d = x.shape
    tm = PROJ_TM
    tok = lambda c: pl.BlockSpec((1, c, tm), lambda i, j: (i, 0, j))
    row = lambda c: pl.BlockSpec((1, tm, c), lambda i, j: (i, j, 0))
    return pl.pallas_call(
        _even_proj_kernel,
        grid=(b, s // tm),
        in_specs=[row(d), _const_spec(gmix.shape), _const_spec(w1t.shape), _const_spec(w2.shape),
                  _const_spec(gaq.shape), _const_spec(gak.shape), _const_spec(gbq.shape),
                  _const_spec(gbkv.shape), _const_spec(wuqt.shape), _const_spec(wk.shape),
                  _const_spec(wvt.shape),
                  pl.BlockSpec((2, HEAD_DIM, tm), lambda i, j: (0, 0, j)),
                  pl.BlockSpec((2, B_ROPE, tm), lambda i, j: (0, 0, j))],
        out_specs=[tok(A_Q), row(A_KV), tok(A_KV), tok(B_HEADS * B_QK_PAD),
                   row(B_HEADS * B_QK_PAD), tok(B_HEADS * B_V)],
        out_shape=[jax.ShapeDtypeStruct((b, A_Q, s), BF16),
                   jax.ShapeDtypeStruct((b, s, A_KV), BF16),
                   jax.ShapeDtypeStruct((b, A_KV, s), BF16),
                   jax.ShapeDtypeStruct((b, B_HEADS * B_QK_PAD, s), BF16),
                   jax.ShapeDtypeStruct((b, s, B_HEADS * B_QK_PAD), BF16),
                   jax.ShapeDtypeStruct((b, B_HEADS * B_V, s), BF16)],
        compiler_params=pltpu.CompilerParams(
            dimension_semantics=("arbitrary", "arbitrary"), vmem_limit_bytes=VMEM_LIMIT),
        name="even_proj",
    )(x, gmix, w1t, w2, gaq, gak, gbq, gbkv, wuqt, wk, wvt, ropea, ropeb)


def _flash_kernel(qt_ref, qtn_ref, k_ref, vt_ref, o_ref, qz_ref, m_ref, acc_ref,
                  sa_ref, sb_ref, mxa_ref, mxb_ref, *, gqa, tk, q_axis):
    seq = k_ref.shape[1]
    tq = qt_ref.shape[2]
    m_cols = qz_ref.shape[3]
    dv = HEAD_DIM
    n = seq // tk
    assert n % 2 == 0
    jq = pl.program_id(q_axis)
    slot = jq % 2

    def stage(src_ref, dst_slot):
        for st in range(2):
            if gqa:
                g = m_cols // tq
                qcat = jnp.concatenate(
                    [src_ref[0, (st * g + i) * HEAD_DIM:(st * g + i + 1) * HEAD_DIM, :] for i in range(g)],
                    axis=1)
                zeros = jnp.zeros_like(qcat)
                qz_ref[dst_slot, st] = jnp.concatenate([qcat, zeros] if st == 0 else [zeros, qcat], axis=0)
            else:
                qz_ref[dst_slot, st] = src_ref[0, st * B_QK_PAD:(st + 1) * B_QK_PAD, :]

    slabs = [(st, sl * FLASH_SLAB) for st in range(2) for sl in range(m_cols // FLASH_SLAB)]

    def k_chunk(c, st):
        off = pl.multiple_of(c * tk, tk)
        if gqa:
            return k_ref[0, pl.ds(off, tk), :]
        return k_ref[0, pl.ds(off, tk), st * B_QK_PAD:(st + 1) * B_QK_PAD]

    ones = jnp.ones((ACC_ROWS - dv, tk), BF16)

    def v_chunk(c, st):
        off = pl.multiple_of(c * tk, tk)
        return jnp.concatenate([vt_ref[0, st * dv:(st + 1) * dv, pl.ds(off, tk)], ones], axis=0)

    def scores(kc, q_slot, st, col, s_ref, mx_ref):
        cols = slice(col, col + FLASH_SLAB)
        s = _dot(kc, qz_ref[q_slot, st, :, cols])
        s_ref[st, col // FLASH_SLAB] = s
        mx_ref[st, :, cols] = jnp.max(s, axis=0, keepdims=True)

    def softmax_pv(vc, st, col, s_ref, mx_ref):
        cols = slice(col, col + FLASH_SLAB)
        m_prev = m_ref[st, :, cols]
        m_new = jnp.maximum(m_prev, mx_ref[st, :, cols])
        alpha = jnp.exp2(m_prev - m_new)
        p = jnp.exp2(s_ref[st, col // FLASH_SLAB] - m_new).astype(BF16)
        acc_ref[st, :, cols] = alpha * acc_ref[st, :, cols] + _dot(vc, p)
        m_ref[st, :, cols] = m_new

    def step(c_next, q_slot, nxt, c_cur, cur):
        kcs = [k_chunk(c_next, st) for st in range(2)] if nxt is not None else None
        vcs = [v_chunk(c_cur, st) for st in range(2)] if cur is not None else None
        for st, col in slabs:
            if nxt is not None:
                scores(kcs[st], q_slot, st, col, *nxt)
            if cur is not None:
                softmax_pv(vcs[st], st, col, *cur)

    bufs = ((sa_ref, mxa_ref), (sb_ref, mxb_ref))

    @pl.when(jq == 0)
    def _():
        stage(qt_ref, 0)
        step(0, 0, bufs[0], None, None)

    stage(qtn_ref, 1 - slot)
    m_ref[...] = jnp.full(m_ref.shape, NEG, F32)
    acc_ref[...] = jnp.zeros(acc_ref.shape, F32)
    iters = (n - 1) // FLASH_UNROLL

    def body(i, carry):
        c0 = 1 + FLASH_UNROLL * i
        for u in range(FLASH_UNROLL):
            step(c0 + u, slot, bufs[(1 + u) % 2], c0 + u - 1, bufs[u % 2])
        return carry

    if iters:
        lax.fori_loop(0, iters, body, 0)
    for c in range(1 + FLASH_UNROLL * iters, n):
        step(c, slot, bufs[c % 2], c - 1, bufs[(c - 1) % 2])
    step(0, 1 - slot, bufs[0], n - 1, bufs[(n - 1) % 2])

    for st in range(2):
        acc = acc_ref[st]
        o = acc[:dv] / acc[dv:dv + 1]
        if gqa:
            g = m_cols // tq
            for i in range(g):
                o_ref[0, (st * g + i) * dv:(st * g + i + 1) * dv, :] = o[:, i * tq:(i + 1) * tq].astype(BF16)
        else:
            o_ref[0, st * dv:(st + 1) * dv, :] = o.astype(BF16)


def _flash_scratch(dk):
    return [pltpu.VMEM((2, 2, dk, FLASH_M), BF16),
            pltpu.VMEM((2, 1, FLASH_M), F32),
            pltpu.VMEM((2, ACC_ROWS, FLASH_M), F32),
            pltpu.VMEM((2, FLASH_M // FLASH_SLAB, FLASH_TK, FLASH_SLAB), F32),
            pltpu.VMEM((2, FLASH_M // FLASH_SLAB, FLASH_TK, FLASH_SLAB), F32),
            pltpu.VMEM((2, 1, FLASH_M), F32),
            pltpu.VMEM((2, 1, FLASH_M), F32)]


def _flash_gqa(qt, k, vt):
    b, cq, s = qt.shape
    g = A_HEADS // A_KV_HEADS
    tq = FLASH_M // g
    return pl.pallas_call(
        functools.partial(_flash_kernel, gqa=True, tk=FLASH_TK, q_axis=1),
        grid=(b, s // tq),
        in_specs=[pl.BlockSpec((1, cq, tq), lambda i, j: (i, 0, j)),
                  pl.BlockSpec((1, cq, tq), lambda i, j: (i, 0, jnp.minimum(j + 1, s // tq - 1))),
                  pl.BlockSpec((1, s, A_KV), lambda i, j: (i, 0, 0)),
                  pl.BlockSpec((1, A_KV, s), lambda i, j: (i, 0, 0))],
        out_specs=pl.BlockSpec((1, cq, tq), lambda i, j: (i, 0, j)),
        out_shape=jax.ShapeDtypeStruct((b, cq, s), BF16),
        scratch_shapes=_flash_scratch(2 * HEAD_DIM),
        compiler_params=pltpu.CompilerParams(
            dimension_semantics=("arbitrary", "arbitrary"), vmem_limit_bytes=VMEM_LIMIT),
        name="flash_gqa",
    )(qt, qt, k, vt)


def _flash_mla(qt, k, vt):
    b, _, s = qt.shape
    tq = FLASH_M
    pairs = B_HEADS // 2
    return pl.pallas_call(
        functools.partial(_flash_kernel, gqa=False, tk=FLASH_TK, q_axis=2),
        grid=(b, pairs, s // tq),
        in_specs=[pl.BlockSpec((1, 2 * B_QK_PAD, tq), lambda i, h, j: (i, h, j)),
                  pl.BlockSpec((1, 2 * B_QK_PAD, tq), lambda i, h, j: (i, h, jnp.minimum(j + 1, s // tq - 1))),
                  pl.BlockSpec((1, s, 2 * B_QK_PAD), lambda i, h, j: (i, 0, h)),
                  pl.BlockSpec((1, 2 * B_V, s), lambda i, h, j: (i, h, 0))],
        out_specs=pl.BlockSpec((1, 2 * B_V, tq), lambda i, h, j: (i, h, j)),
        out_shape=jax.ShapeDtypeStruct((b, B_HEADS * B_V, s), BF16),
        scratch_shapes=_flash_scratch(B_QK_PAD),
        compiler_params=pltpu.CompilerParams(
            dimension_semantics=("arbitrary", "arbitrary", "arbitrary"), vmem_limit_bytes=VMEM_LIMIT),
        name="flash_mla",
    )(qt, qt, k, vt)


def _mlp_kernel(*refs, n_mix, final):
    h_ref = refs[0]
    mix_refs = refs[1:1 + 2 * n_mix]
    gffn_ref, wup_ref, wdown_ref = refs[1 + 2 * n_mix:4 + 2 * n_mix]
    rest = refs[4 + 2 * n_mix:]
    gfin_ref = rest[0] if final else None
    o_ref = rest[-1]

    at = None
    for i in range(n_mix):
        part = lax.dot_general(mix_refs[2 * i + 1][0], mix_refs[2 * i][...], (((0,), (1,)), ((), ())),
                               preferred_element_type=F32)
        at = part if at is None else at + part
    h1 = h_ref[0] + at
    hn = _rms_rows(h1, gffn_ref[...]).astype(BF16)
    acc = h1
    n_chunks = D_FF // D_MODEL
    for c in range(n_chunks):
        u = jnp.maximum(_dot(hn, wup_ref[:, c * D_MODEL:(c + 1) * D_MODEL]), 0.0)
        acc = acc + _dot((u * u).astype(BF16), wdown_ref[c * D_MODEL:(c + 1) * D_MODEL, :])
    if final:
        acc = _rms_rows(acc, gfin_ref[...])
    o_ref[0] = acc


def _outproj_mlp(h, mixes, gffn, wup, wdown, gfin=None):
    b, s, d = h.shape
    tm = MLP_TM
    args = [h]
    in_specs = [pl.BlockSpec((1, tm, d), lambda i, j: (i, j, 0))]
    for wt, ot in mixes:
        args += [wt, ot]
        in_specs += [_const_spec(wt.shape), pl.BlockSpec((1, ot.shape[1], tm), lambda i, j: (i, 0, j))]
    args += [gffn, wup, wdown]
    in_specs += [_const_spec(gffn.shape), _const_spec(wup.shape), _const_spec(wdown.shape)]
    if gfin is not None:
        args.append(gfin)
        in_specs.append(_const_spec(gfin.shape))
    return pl.pallas_call(
        functools.partial(_mlp_kernel, n_mix=len(mixes), final=gfin is not None),
        grid=(b, s // tm),
        in_specs=in_specs,
        out_specs=pl.BlockSpec((1, tm, d), lambda i, j: (i, j, 0)),
        out_shape=jax.ShapeDtypeStruct((b, s, d), F32),
        compiler_params=pltpu.CompilerParams(
            dimension_semantics=("arbitrary", "arbitrary"), vmem_limit_bytes=VMEM_LIMIT),
        name="outproj_mlp",
    )(*args)


def _odd_proj_kernel(h_ref, g_ref, wqt_ref, wk_ref, wvt_ref, qt_ref, k_ref, vt_ref):
    hn = _rms_rows(h_ref[0], g_ref[...]).astype(BF16)
    scale = HEAD_DIM ** -0.5 * LOG2E
    qt_ref[0] = (_nt_dot(wqt_ref[...], hn) * scale).astype(BF16)
    k_ref[0] = _dot(hn, wk_ref[...]).astype(BF16)
    vt_ref[0] = _nt_dot(wvt_ref[...], hn).astype(BF16)


def _odd_proj(h, g, wqt, wk, wvt):
    b, s, d = h.shape
    tm = PROJ_TM
    c = wqt.shape[0]
    tok = pl.BlockSpec((1, c, tm), lambda i, j: (i, 0, j))
    row = pl.BlockSpec((1, tm, c), lambda i, j: (i, j, 0))
    return pl.pallas_call(
        _odd_proj_kernel,
        grid=(b, s // tm),
        in_specs=[pl.BlockSpec((1, tm, d), lambda i, j: (i, j, 0)), _const_spec(g.shape),
                  _const_spec(wqt.shape), _const_spec(wk.shape), _const_spec(wvt.shape)],
        out_specs=[tok, row, tok],
        out_shape=[jax.ShapeDtypeStruct((b, c, s), BF16),
                   jax.ShapeDtypeStruct((b, s, c), BF16),
                   jax.ShapeDtypeStruct((b, c, s), BF16)],
        compiler_params=pltpu.CompilerParams(
            dimension_semantics=("arbitrary", "arbitrary"), vmem_limit_bytes=VMEM_LIMIT),
        name="odd_proj",
    )(h, g, wqt, wk, wvt)


def _nat_block_geometry(rows):
    nblk = rows // NAT_QROWS
    kinds = []
    for blk in (0, 1, nblk - 1):
        r0 = blk * NAT_QROWS
        ws = min(max(r0 - C_WIN_ROWS // 2, 0), rows - NAT_KROWS)
        table = []
        for kr_l in range(NAT_KROWS):
            line = []
            for qr_l in range(NAT_QROWS):
                qr, kr = r0 + qr_l, ws + kr_l
                rs = min(max(qr - C_WIN_ROWS // 2, 0), rows - C_WIN_ROWS)
                line.append(kr - qr + C_WIN_ROWS - 1 if rs <= kr < rs + C_WIN_ROWS else None)
            table.append(line)
        kinds.append(table)
    return kinds


def _nat_kernel(qt_ref, k_ref, vt_ref, tab_ref, o_ref, bias_ref, sa_ref, sb_ref, mxa_ref, mxb_ref,
                *, rows):
    w = GRID_W
    nq = NAT_QROWS * w
    nblk = rows // NAT_QROWS
    assert nblk >= 4 and nblk % 2 == 0 and rows >= NAT_KROWS
    kinds = _nat_block_geometry(rows)

    @pl.when(pl.program_id(1) == 0)
    def _():
        lane = lax.broadcasted_iota(jnp.int32, (w, 2 * w), 1)
        neg = jnp.full((w, 2 * w), NEG, F32)
        for j in range(2):
            for t, table in enumerate(kinds):
                for kr_l in range(NAT_KROWS):
                    for qp in range(NAT_QROWS // 2):
                        ia, ib = table[kr_l][2 * qp], table[kr_l][2 * qp + 1]
                        ta = neg if ia is None else tab_ref[j, ia]
                        tb = neg if ib is None else tab_ref[j, ib]
                        tile = neg if (ia is None and ib is None) else jnp.where(lane < w, ta, tb)
                        bias_ref[j, t, kr_l * w:(kr_l + 1) * w, qp * 2 * w:(qp + 1) * 2 * w] = tile

    nqp = NAT_QROWS // 2
    lw = 2 * w
    assert lw == V7X_LANES

    def live(kind, kr_l, qp):
        line = kinds[kind][kr_l]
        return line[2 * qp] is not None or line[2 * qp + 1] is not None

    def key_rows(kind):
        rows_live = [r for r in range(NAT_KROWS) if any(live(kind, r, qp) for qp in range(nqp))]
        lo, hi = rows_live[0], rows_live[-1] + 1
        assert rows_live == list(range(lo, hi))
        hi_pv = lo + 2 * ((hi - lo + 1) // 2)
        assert hi_pv <= NAT_KROWS
        return lo, hi, hi_pv

    def scores(offs, kind, j, s_ref, mx_ref):
        q_off, k_off = offs
        lo, hi, _ = key_rows(kind)
        kw = k_ref[0, pl.ds(k_off + lo * w, (hi - lo) * w), :]
        q = qt_ref[0, j * HEAD_DIM:(j + 1) * HEAD_DIM, pl.ds(q_off, nq)]
        zeros = jnp.zeros_like(q)
        qz = jnp.concatenate([q, zeros] if j == 0 else [zeros, q], axis=0)
        s = _dot(kw, qz)
        for qp in range(nqp):
            cols = slice(qp * lw, (qp + 1) * lw)
            mx = None
            for kr_l in range(lo, hi):
                if live(kind, kr_l, qp):
                    r = slice(kr_l * w, (kr_l + 1) * w)
                    blk = s[(kr_l - lo) * w:(kr_l - lo + 1) * w, cols] + bias_ref[j, kind, r, cols]
                    s_ref[j, r, cols] = blk
                    mx = blk if mx is None else jnp.maximum(mx, blk)
            mx_ref[j, :, cols] = jnp.max(mx, axis=0, keepdims=True)

    def softmax_pv(offs, kind, j, s_ref, mx_ref):
        q_off, k_off = offs
        lo, _, hi_pv = key_rows(kind)
        p_cols = []
        for qp in range(nqp):
            cols = slice(qp * lw, (qp + 1) * lw)
            m = mx_ref[j, :, cols]
            pieces = []
            for kr_l in range(lo, hi_pv):
                if live(kind, kr_l, qp):
                    pieces.append(jnp.exp2(s_ref[j, kr_l * w:(kr_l + 1) * w, cols] - m).astype(BF16))
                else:
                    pieces.append(jnp.zeros((w, lw), BF16))
            p_cols.append(jnp.concatenate(pieces, axis=0))
        p = jnp.concatenate(p_cols, axis=1)
        nkp = (hi_pv - lo) * w
        vw = jnp.concatenate(
            [vt_ref[0, j * HEAD_DIM:(j + 1) * HEAD_DIM, pl.ds(k_off + lo * w, nkp)],
             jnp.ones((ACC_ROWS - HEAD_DIM, nkp), BF16)], axis=0)
        pv = _dot(vw, p)
        o = pv[:HEAD_DIM] / pv[HEAD_DIM:HEAD_DIM + 1]
        o_ref[0, j * HEAD_DIM:(j + 1) * HEAD_DIM, pl.ds(q_off, nq)] = o.astype(BF16)

    half = (C_WIN_ROWS // 2) * w

    def desc(t):
        if isinstance(t, int):
            if t == 0:
                return (0, 0), 0
            if t == nblk - 1:
                return ((nblk - 1) * nq, (rows - NAT_KROWS) * w), 2
            return (t * nq, t * nq - half), 1
        return (pl.multiple_of(t * nq, nq), pl.multiple_of(t * nq - half, half)), 1

    bufs = ((sa_ref, mxa_ref), (sb_ref, mxb_ref))

    def step(t_next, par_next, t_cur):
        for j in range(2):
            if t_next is not None:
                offs, kind = desc(t_next)
                scores(offs, kind, j, *bufs[par_next])
            if t_cur is not None:
                offs, kind = desc(t_cur)
                softmax_pv(offs, kind, j, *bufs[1 - par_next])

    step(0, 0, None)
    step(1, 1, 0)
    iters = (nblk - 3) // NAT_UNROLL

    def body(i, carry):
        t0 = 2 + NAT_UNROLL * i
        for u in range(NAT_UNROLL):
            step(t0 + u, u % 2, t0 + u - 1)
        return carry

    if iters:
        lax.fori_loop(0, iters, body, 0)
    for t in range(2 + NAT_UNROLL * iters, nblk):
        step(t, t % 2, t - 1)
    step(None, nblk % 2, nblk - 1)


def _natten(qt, k, vt, tab):
    b, c, s = qt.shape
    rows = s // GRID_W
    pairs = C_HEADS // 2
    pc = 2 * HEAD_DIM
    nrel = 2 * C_WIN_ROWS - 1
    return pl.pallas_call(
        functools.partial(_nat_kernel, rows=rows),
        grid=(pairs, b),
        in_specs=[pl.BlockSpec((1, pc, s), lambda h, i: (i, h, 0)),
                  pl.BlockSpec((1, s, pc), lambda h, i: (i, 0, h)),
                  pl.BlockSpec((1, pc, s), lambda h, i: (i, h, 0)),
                  pl.BlockSpec((2, nrel, GRID_W, 2 * GRID_W), lambda h, i: (h, 0, 0, 0))],
        out_specs=pl.BlockSpec((1, pc, s), lambda h, i: (i, h, 0)),
        out_shape=jax.ShapeDtypeStruct((b, c, s), BF16),
        scratch_shapes=[pltpu.VMEM((2, 3, NAT_KROWS * GRID_W, NAT_QROWS * GRID_W), F32),
                        pltpu.VMEM((2, NAT_KROWS * GRID_W, NAT_QROWS * GRID_W), F32),
                        pltpu.VMEM((2, NAT_KROWS * GRID_W, NAT_QROWS * GRID_W), F32),
                        pltpu.VMEM((2, 1, NAT_QROWS * GRID_W), F32),
                        pltpu.VMEM((2, 1, NAT_QROWS * GRID_W), F32)],
        compiler_params=pltpu.CompilerParams(
            dimension_semantics=("arbitrary", "arbitrary"), vmem_limit_bytes=VMEM_LIMIT),
        name="natten",
    )(qt, k, vt, tab)


def _nat_bias_tiles(rpb):
    kc = np.arange(GRID_W)[:, None]
    qc = np.arange(GRID_W)[None, :]
    cs = np.clip(qc - C_WIN_COLS // 2, 0, GRID_W - C_WIN_COLS)
    valid = (kc >= cs) & (kc < cs + C_WIN_COLS)
    rel = kc - qc + (C_WIN_COLS - 1)
    onehot = (rel[None] == np.arange(2 * C_WIN_COLS - 1)[:, None, None]).astype(np.float32)
    tiles = jnp.einsum("had,dkq->hakq", rpb * LOG2E, onehot, precision=lax.Precision.HIGHEST)
    tiles = jnp.where(valid[None, None], tiles, NEG)
    return jnp.concatenate([tiles, tiles], axis=-1).astype(F32)


def _even_layer_weights(w_in, w_uq, w_ukv):
    qa, ka, va, cq, ckv, kr = jnp.split(
        w_in, (A_Q, A_Q + A_KV, A_Q + 2 * A_KV, A_Q + 2 * A_KV + B_Q_RANK,
               A_Q + 2 * A_KV + B_Q_RANK + B_KV_RANK), axis=1)
    w1t = jnp.concatenate([qa, ka, va, cq, kr], axis=1).T.astype(BF16)
    w2 = ckv.astype(BF16)
    uq = w_uq.reshape(B_Q_RANK, B_HEADS, B_NOPE + B_ROPE)
    uq = jnp.pad(uq, ((0, 0), (0, 0), (0, B_QK_PAD - B_NOPE - B_ROPE)))
    wuqt = uq.reshape(B_Q_RANK, B_HEADS * B_QK_PAD).T.astype(BF16)
    ukv = w_ukv.reshape(B_KV_RANK, B_HEADS, B_NOPE + B_V)
    wk = jnp.pad(ukv[:, :, :B_NOPE], ((0, 0), (0, 0), (0, B_QK_PAD - B_NOPE)))
    wk = wk.reshape(B_KV_RANK, B_HEADS * B_QK_PAD).astype(BF16)
    wvt = ukv[:, :, B_NOPE:].reshape(B_KV_RANK, B_HEADS * B_V).T.astype(BF16)
    return w1t, w2, wuqt, wk, wvt


def kernel(x, norm_mix, ev_w_in, ev_a_q_norm, ev_a_k_norm, ev_b_q_norm, ev_b_w_uq, ev_b_kv_norm,
           ev_b_w_ukv, ev_w_out, od_w_qkv, od_rpb, od_w_out, norm_ffn, ffn_w_up, ffn_w_down,
           final_norm):
    depth = norm_mix.shape[0]
    s = x.shape[1]
    ropea = _rope_tables(s, HEAD_DIM)
    ropeb = _rope_tables(s, B_ROPE)
    h = x
    for layer in range(depth):
        i = layer // 2
        gmix = norm_mix[layer][None, :]
        if layer % 2 == 0:
            w1t, w2, wuqt, wk, wvt = _even_layer_weights(ev_w_in[i], ev_b_w_uq[i], ev_b_w_ukv[i])
            qat, ka, vat, qbt, kb, vbt = _even_proj(
                h, gmix, w1t, w2, ev_a_q_norm[i][:, None], ev_a_k_norm[i][:, None],
                ev_b_q_norm[i][:, None], ev_b_kv_norm[i][None, :], wuqt, wk, wvt, ropea, ropeb)
            oat = _flash_gqa(qat, ka, vat)
            obt = _flash_mla(qbt, kb, vbt)
            wot = ev_w_out[i].T.astype(BF16)
            mixes = [(wot[:, :A_Q], oat), (wot[:, A_Q:], obt)]
        else:
            wq, wk_, wv = jnp.split(od_w_qkv[i], 3, axis=1)
            qt, k, vt = _odd_proj(h, gmix, wq.T.astype(BF16), wk_.astype(BF16), wv.T.astype(BF16))
            ot = _natten(qt, k, vt, _nat_bias_tiles(od_rpb[i]))
            mixes = [(od_w_out[i].T.astype(BF16), ot)]
        last = layer == depth - 1
        h = _outproj_mlp(h, mixes, norm_ffn[layer][None, :], ffn_w_up[layer].astype(BF16),
                         ffn_w_down[layer].astype(BF16), final_norm[None, :] if last else None)
    return h
```
